```python
import math
import jax, jax.numpy as jnp
from jax import lax
import numpy as np

D_MODEL = 2048
BATCH = 4
SEQ = 4096
DEPTH = 2

CHUNK = 64

N_A = DEPTH // 2
N_B = DEPTH - N_A

CONV_WIDTH = 31

SB_HEADS = 16
SB_HEAD_DIM = D_MODEL // SB_HEADS
Q_BLOCK = 128

PEER_HEADS = 8
N_KEYS = 128
N_EXPERTS = N_KEYS * N_KEYS
PEER_TOPK = 16
D_QUERY = 256
D_QHALF = D_QUERY // 2
TOKEN_BLOCK = 128

ALPHA = (2.0 * DEPTH) ** 0.25
DEEPNORM_BETA = (8.0 * DEPTH) ** -0.25
LN_EPS = 1e-5

kernel_name = "yoco_conformer_stickbreaking_peer"


def layer_norm(x, g, b):
    xf = x.astype(jnp.float32)
    mu = jnp.mean(xf, axis=-1, keepdims=True)
    var = jnp.mean(jnp.square(xf - mu), axis=-1, keepdims=True)
    y = (xf - mu) * lax.rsqrt(var + LN_EPS)
    return (y * g + b).astype(x.dtype)


def conformer_conv(x, w_in, b_in, w_dw, b_dw, ln_g, ln_b, w_out, b_out):
    h = x @ w_in + b_in
    a, gate = jnp.split(h, 2, axis=-1)
    h = a * jax.nn.sigmoid(gate)
    h = lax.conv_general_dilated(
        h, w_dw.astype(h.dtype), window_strides=(1,), padding=[(CONV_WIDTH - 1, 0)],
        dimension_numbers=("NWC", "WIO", "NWC"), feature_group_count=D_MODEL) + b_dw
    h = layer_norm(h, ln_g, ln_b)
    h = jax.nn.swish(h)
    return h @ w_out + b_out


def split_heads(t):
    b, s, _ = t.shape
    return t.reshape(b, s, SB_HEADS, SB_HEAD_DIM).transpose(0, 2, 1, 3)


def stick_breaking_attention(q, k, v):
    seq = q.shape[2]
    scale = SB_HEAD_DIM ** -0.5
    outs = []
    for blk in range(seq // Q_BLOCK):
        t0, t1 = blk * Q_BLOCK, (blk + 1) * Q_BLOCK
        qb = q[:, :, t0:t1]
        kb = k[:, :, :t1]
        vb = v[:, :, :t1]
        z = jnp.einsum("bhtd,bhsd->bhts", qb, kb,
                       preferred_element_type=jnp.float32) * scale
        t_idx = t0 + jnp.arange(Q_BLOCK)[:, None]
        s_idx = jnp.arange(t1)[None, :]
        mask = s_idx < t_idx
        log_one_minus = jnp.where(mask, jax.nn.log_sigmoid(-z), 0.0)
        log_rest = lax.cumsum(log_one_minus, axis=3, reverse=True) - log_one_minus
        att = jnp.where(mask, jnp.exp(jax.nn.log_sigmoid(z) + log_rest), 0.0)
        outs.append(jnp.einsum("bhts,bhsd->bhtd", att.astype(vb.dtype), vb))
    return jnp.concatenate(outs, axis=2)


def peer(x, w_q, sub_keys, u_tab, v_tab):
    b, s, d = x.shape
    n_tok = b * s
    xt = x.reshape(n_tok, d)
    q = (xt @ w_q).reshape(n_tok, PEER_HEADS, 2, D_QHALF)
    sc = jnp.einsum("thcd,hckd->thck", q, sub_keys,
                    preferred_element_type=jnp.float32)
    top_s, top_i = lax.top_k(sc, PEER_TOPK)
    cand_s = (top_s[:, :, 0, :, None] + top_s[:, :, 1, None, :]).reshape(
        n_tok, PEER_HEADS, PEER_TOPK * PEER_TOPK)
    cand_e = (top_i[:, :, 0, :, None] * N_KEYS + top_i[:, :, 1, None, :]).reshape(
        n_tok, PEER_HEADS, PEER_TOPK * PEER_TOPK)
    best_s, best_j = lax.top_k(cand_s, PEER_TOPK)
    experts = jnp.take_along_axis(cand_e, best_j, axis=-1)
    gates = jax.nn.softmax(best_s, axis=-1).astype(x.dtype)

    n_blk = n_tok // TOKEN_BLOCK

    def block_fn(args):
        xb, eb, gb = args
        u = u_tab[eb]
        h = jnp.einsum("thkd,td->thk", u, xb)
        coef = gb * jax.nn.gelu(h, approximate=False)
        return jnp.einsum("thk,thkd->td", coef, v_tab[eb])

    out = lax.map(block_fn, (xt.reshape(n_blk, TOKEN_BLOCK, d),
                             experts.reshape(n_blk, TOKEN_BLOCK, PEER_HEADS, PEER_TOPK),
                             gates.reshape(n_blk, TOKEN_BLOCK, PEER_HEADS, PEER_TOPK)))
    return out.reshape(b, s, d)


def setup_inputs(seed: int = 0) -> dict:
    key = jax.random.key(seed)
    ks = jax.random.split(key, 20)
    D = D_MODEL
    f32 = jnp.float32
    nrm = lambda k, shape, scale: jax.random.normal(k, shape, f32) * scale
    return {
        "x": jax.random.normal(ks[0], (BATCH, SEQ, D), f32),
        "conv_w_in": nrm(ks[1], (N_A, D, 2 * D), D ** -0.5),
        "conv_b_in": nrm(ks[2], (N_A, 2 * D), 0.02),
        "conv_w_dw": nrm(ks[3], (N_A, CONV_WIDTH, 1, D), CONV_WIDTH ** -0.5),
        "conv_b_dw": nrm(ks[4], (N_A, D), 0.02),
        "conv_ln_g": 1.0 + nrm(ks[5], (N_A, D), 0.02),
        "conv_ln_b": nrm(ks[6], (N_A, D), 0.02),
        "conv_w_out": nrm(ks[7], (N_A, D, D), D ** -0.5 * DEEPNORM_BETA),
        "conv_b_out": nrm(ks[8], (N_A, D), 0.02),
        "attn_w_kv": jnp.concatenate([
            nrm(ks[9], (D, D), D ** -0.5),
            nrm(ks[10], (D, D), D ** -0.5 * DEEPNORM_BETA)], axis=1),
        "attn_w_q": nrm(ks[11], (N_B, D, D), D ** -0.5),
        "attn_w_o": nrm(ks[12], (N_B, D, D), D ** -0.5 * DEEPNORM_BETA),
        "peer_w_q": nrm(ks[13], (DEPTH, D, PEER_HEADS * D_QUERY), D ** -0.5),
        "peer_sub_keys": nrm(ks[14], (DEPTH, PEER_HEADS, 2, N_KEYS, D_QHALF), D_QHALF ** -0.5),
        "peer_u": nrm(ks[15], (DEPTH, N_EXPERTS, D), D ** -0.5),
        "peer_v": nrm(ks[16], (DEPTH, N_EXPERTS, D), DEEPNORM_BETA),
        "ln_g": 1.0 + nrm(ks[17], (DEPTH, 2, D), 0.02),
        "ln_b": nrm(ks[18], (DEPTH, 2, D), 0.02),
    }


def reference(x, conv_w_in, conv_b_in, conv_w_dw, conv_b_dw, conv_ln_g, conv_ln_b,
              conv_w_out, conv_b_out, attn_w_kv, attn_w_q, attn_w_o,
              peer_w_q, peer_sub_keys, peer_u, peer_v, ln_g, ln_b):
    k_shared = None
    v_shared = None
    for i in range(DEPTH):
        if i < N_A:
            y = conformer_conv(x, conv_w_in[i], conv_b_in[i], conv_w_dw[i], conv_b_dw[i],
                               conv_ln_g[i], conv_ln_b[i], conv_w_out[i], conv_b_out[i])
        else:
            j = i - N_A
            q = split_heads(x @ attn_w_q[j])
            o = stick_breaking_attention(q, k_shared, v_shared)
            b, _, s, _ = o.shape
            y = o.transpose(0, 2, 1, 3).reshape(b, s, D_MODEL) @ attn_w_o[j]
        x = layer_norm(ALPHA * x + y, ln_g[i, 0], ln_b[i, 0])
        x = layer_norm(ALPHA * x + peer(x, peer_w_q[i], peer_sub_keys[i], peer_u[i], peer_v[i]),
                       ln_g[i, 1], ln_b[i, 1])
        if i == N_A - 1:
            kv = x @ attn_w_kv
            k_part, v_part = jnp.split(kv, 2, axis=-1)
            k_shared = split_heads(k_part)
            v_shared = split_heads(v_part)
    return x
```

```python
import functools
import math

import jax
import jax.numpy as jnp
from jax import lax
from jax.experimental import pallas as pl
from jax.experimental.pallas import tpu as pltpu

F32 = jnp.float32
BF16 = jnp.bfloat16

LN_EPS = 1e-5
CONV_HALO = 32
PEER_TOPK = 16
SB_HEAD_DIM = 128
VMEM_LIMIT_BYTES = 56 * 1024 * 1024


def _cparams(*sem):
    return pltpu.CompilerParams(dimension_semantics=sem, vmem_limit_bytes=VMEM_LIMIT_BYTES)


def _layer_norm(v, g, b):
    mu = jnp.mean(v, axis=-1, keepdims=True)
    c = v - mu
    var = jnp.mean(c * c, axis=-1, keepdims=True)
    return c * lax.rsqrt(var + LN_EPS) * g + b


def _mm_body(x_ref, w_ref, o_ref):
    o_ref[...] = jnp.dot(x_ref[...], w_ref[...], preferred_element_type=F32).astype(o_ref.dtype)


def _matmul(x, w, out_dtype, tm=512, tn=512, name="matmul"):
    m, k = x.shape
    n = w.shape[1]
    tm, tn = min(tm, m), min(tn, n)
    return pl.pallas_call(
        _mm_body,
        grid=(m // tm, n // tn),
        in_specs=[pl.BlockSpec((tm, k), lambda i, j: (i, 0)),
                  pl.BlockSpec((k, tn), lambda i, j: (0, j))],
        out_specs=pl.BlockSpec((tm, tn), lambda i, j: (i, j)),
        out_shape=jax.ShapeDtypeStruct((m, n), out_dtype),
        compiler_params=_cparams("parallel", "parallel"),
        name=name,
    )(x, w)


def _glu_body(x_ref, wa_ref, wg_ref, ba_ref, bg_ref, o_ref):
    xb = x_ref[...].astype(BF16)
    a = jnp.dot(xb, wa_ref[...], preferred_element_type=F32) + ba_ref[...]
    gate = jnp.dot(xb, wg_ref[...], preferred_element_type=F32) + bg_ref[...]
    o_ref[...] = a * jax.nn.sigmoid(gate)


def _glu(x, w_in, b_in, tm=512, tn=512):
    t, d = x.shape
    tm, tn = min(tm, t), min(tn, d)
    nb = d // tn
    b2 = b_in.reshape(1, 2 * d)
    return pl.pallas_call(
        _glu_body,
        grid=(t // tm, nb),
        in_specs=[pl.BlockSpec((tm, d), lambda i, j: (i, 0)),
                  pl.BlockSpec((d, tn), lambda i, j: (0, j)),
                  pl.BlockSpec((d, tn), lambda i, j: (0, j + nb)),
                  pl.BlockSpec((1, tn), lambda i, j: (0, j)),
                  pl.BlockSpec((1, tn), lambda i, j: (0, j + nb))],
        out_specs=pl.BlockSpec((tm, tn), lambda i, j: (i, j)),
        out_shape=jax.ShapeDtypeStruct((t, d), F32),
        compiler_params=_cparams("parallel", "parallel"),
        name="conv_glu",
    )(x, w_in, w_in, b2, b2)


def _dwconv_body(halo_ref, cur_ref, w_ref, b_ref, o_ref, buf_ref, *, width, ts):
    i = pl.program_id(1)
    halo = halo_ref[...]
    buf_ref[0:CONV_HALO, :] = jnp.where(i > 0, halo, jnp.zeros_like(halo))
    buf_ref[CONV_HALO:CONV_HALO + ts, :] = cur_ref[...]
    acc = jnp.broadcast_to(b_ref[...], o_ref.shape)
    first = CONV_HALO - (width - 1)
    for k in range(width):
        acc = acc + w_ref[k:k + 1, :] * buf_ref[first + k:first + k + ts, :]
    o_ref[...] = acc


def _dwconv(g3, w_dw, b_dw, ts=512, td=256):
    b, s, d = g3.shape
    width = w_dw.shape[0]
    assert width - 1 <= CONV_HALO
    ts, td = min(ts, s), min(td, d)
    per = ts // CONV_HALO
    return pl.pallas_call(
        functools.partial(_dwconv_body, width=width, ts=ts),
        grid=(b, s // ts, d // td),
        in_specs=[pl.BlockSpec((None, CONV_HALO, td), lambda bi, i, j: (bi, jnp.maximum(i * per - 1, 0), j)),
                  pl.BlockSpec((None, ts, td), lambda bi, i, j: (bi, i, j)),
                  pl.BlockSpec((width, td), lambda bi, i, j: (0, j)),
                  pl.BlockSpec((1, td), lambda bi, i, j: (0, j))],
        out_specs=pl.BlockSpec((None, ts, td), lambda bi, i, j: (bi, i, j)),
        out_shape=jax.ShapeDtypeStruct((b, s, d), F32),
        scratch_shapes=[pltpu.VMEM((CONV_HALO + ts, td), F32)],
        compiler_params=_cparams("parallel", "parallel", "parallel"),
        name="conv_dw",
    )(g3, g3, w_dw, b_dw.reshape(1, d))


def _proj_ln_body(*refs, alpha, pre_ln):
    if pre_ln:
        h_ref, x_ref, w_ref, pg_ref, pb_ref, bias_ref, g_ref, b_ref, o_ref, ot_ref = refs
        h = _layer_norm(h_ref[...], pg_ref[...], pb_ref[...])
        h = (h * jax.nn.sigmoid(h)).astype(BF16)
    else:
        h_ref, x_ref, w_ref, bias_ref, g_ref, b_ref, o_ref, ot_ref = refs
        h = h_ref[...]
    y = jnp.dot(h, w_ref[...], preferred_element_type=F32) + bias_ref[...]
    x1 = _layer_norm(alpha * x_ref[...] + y, g_ref[...], b_ref[...])
    o_ref[...] = x1
    ot_ref[...] = x1.T.astype(BF16)


def _proj_ln(h, x, w, bias, g, b, alpha, pre=None, tm=256):
    t, d = x.shape
    tm = min(tm, t)
    row = lambda v: v.reshape(1, d)
    tok = pl.BlockSpec((tm, d), lambda i: (i, 0))
    vec = pl.BlockSpec((1, d), lambda i: (0, 0))
    mat = pl.BlockSpec((d, d), lambda i: (0, 0))
    if pre is not None:
        ins = [h, x, w, row(pre[0]), row(pre[1]), row(bias), row(g), row(b)]
        specs = [tok, tok, mat, vec, vec, vec, vec, vec]
    else:
        ins = [h, x, w, row(bias), row(g), row(b)]
        specs = [tok, tok, mat, vec, vec, vec]
    return pl.pallas_call(
        functools.partial(_proj_ln_body, alpha=alpha, pre_ln=pre is not None),
        grid=(t // tm,),
        in_specs=specs,
        out_specs=[tok, pl.BlockSpec((d, tm), lambda i: (0, i))],
        out_shape=[jax.ShapeDtypeStruct((t, d), F32), jax.ShapeDtypeStruct((d, t), BF16)],
        compiler_params=_cparams("parallel"),
        name="proj_ln",
    )(*ins)


def _top_ranks(s, k):
    n, tm = s.shape
    row = lax.broadcasted_iota(jnp.int32, (n, tm), 0).astype(F32)
    krow = lax.broadcasted_iota(jnp.int32, (k, tm), 0)
    rank = jnp.full((n, tm), float(k), F32)
    vals = jnp.zeros((k, tm), F32)
    for p in range(k):
        m = jnp.max(s, axis=0, keepdims=True)
        first = jnp.min(jnp.where(s == m, row, float(n)), axis=0, keepdims=True)
        hit = row == first
        rank = jnp.where(hit, float(p), rank)
        s = jnp.where(hit, -jnp.inf, s)
        vals = jnp.where(krow == p, m, vals)
    return vals, rank


def _pair_counts(a, b, k):
    tm = a.shape[1]
    half = k // 2
    p_full = lax.broadcasted_iota(jnp.int32, (k, tm), 0).astype(F32)
    p_half = lax.broadcasted_iota(jnp.int32, (half, tm), 0).astype(F32)
    cand = [a + b[0:1]]
    flat = [p_full * k]
    for q in range(1, k):
        lim = k // (q + 1)
        cand.append(jnp.where(p_half < lim, a[0:half] + b[q:q + 1], -jnp.inf))
        flat.append(p_half * k + q)
    cand = jnp.concatenate(cand, axis=0)
    flat = jnp.concatenate(flat, axis=0)
    picked = jnp.zeros(cand.shape, F32)
    z = jnp.zeros((1, tm), F32)
    best = None
    for r in range(k):
        m = jnp.max(cand, axis=0, keepdims=True)
        best = m if r == 0 else best
        first = jnp.min(jnp.where(cand == m, flat, float(k * k)), axis=0, keepdims=True)
        hit = flat == first
        picked = jnp.where(hit, 1.0, picked)
        cand = jnp.where(hit, -jnp.inf, cand)
        z = z + jnp.exp(m - best)
    count_lo = picked[0:half]
    for q in range(1, k):
        count_lo = count_lo + picked[k + (q - 1) * half:k + q * half]
    count = jnp.concatenate([count_lo, picked[half:k]], axis=0)
    return count, z


def _route_body(x_ref, wq_ref, keys_ref, r1_ref, e1_ref, n0_ref, e0_ref, *, n_heads, n_keys, k):
    xb = x_ref[...].astype(BF16)

    def head(h, carry):
        q = jnp.dot(xb, wq_ref[h], preferred_element_type=F32).astype(BF16)
        dq = q.shape[1] // 2
        dn = (((1,), (1,)), ((), ()))
        s0 = lax.dot_general(keys_ref[h, 0], q[:, :dq], dn, preferred_element_type=F32)
        s1 = lax.dot_general(keys_ref[h, 1], q[:, dq:], dn, preferred_element_type=F32)
        a, rank0 = _top_ranks(s0, k)
        b, rank1 = _top_ranks(s1, k)
        count, z = _pair_counts(a, b, k)
        n0 = jnp.zeros_like(s0)
        for p in range(k):
            n0 = jnp.where(rank0 == float(p), count[p:p + 1], n0)
        r1_ref[h] = rank1.astype(BF16)
        e1_ref[h] = (jnp.exp(s1 - b[0:1]) / z).astype(BF16)
        n0_ref[h] = n0
        e0_ref[h] = jnp.exp(s0 - a[0:1])
        return carry

    lax.fori_loop(0, n_heads, head, 0)


def _route(x, wq_h, keys, tm=256):
    t, d = x.shape
    n_heads, _, n_keys, dq = keys.shape
    tm = min(tm, t)
    sel = pl.BlockSpec((n_heads, n_keys, tm), lambda i: (0, 0, i))
    return pl.pallas_call(
        functools.partial(_route_body, n_heads=n_heads, n_keys=n_keys, k=PEER_TOPK),
        grid=(t // tm,),
        in_specs=[pl.BlockSpec((tm, d), lambda i: (i, 0)),
                  pl.BlockSpec((n_heads, d, 2 * dq), lambda i: (0, 0, 0)),
                  pl.BlockSpec((n_heads, 2, n_keys, dq), lambda i: (0, 0, 0, 0))],
        out_specs=[sel, sel, sel, sel],
        out_shape=[jax.ShapeDtypeStruct((n_heads, n_keys, t), BF16),
                   jax.ShapeDtypeStruct((n_heads, n_keys, t), BF16),
                   jax.ShapeDtypeStruct((n_heads, n_keys, t), F32),
                   jax.ShapeDtypeStruct((n_heads, n_keys, t), F32)],
        compiler_params=_cparams("parallel"),
        name="peer_route",
    )(x, wq_h, keys)


def _peer_dense_body(xt_ref, u_ref, vt_ref, r1_ref, e1_ref, n0_ref, e0_ref, o_ref, *, n_heads, n_keys, rows):
    j = pl.program_id(1)

    @pl.when(j == 0)
    def _():
        o_ref[...] = jnp.zeros_like(o_ref)

    ht = jnp.dot(u_ref[...], xt_ref[...], preferred_element_type=F32)
    tb = ht.shape[1]
    coef = []
    for r in range(rows):
        w = jnp.zeros((n_keys, tb), BF16)
        for h in range(n_heads):
            n_row = jnp.broadcast_to(n0_ref[h, r:r + 1, :], (n_keys, tb)).astype(BF16)
            e_row = jnp.broadcast_to(e0_ref[h, r:r + 1, :], (n_keys, tb)).astype(BF16)
            w = w + jnp.where(r1_ref[h] < n_row, e1_ref[h], jnp.zeros((), BF16)) * e_row
        hr = ht[r * n_keys:(r + 1) * n_keys]
        act = 0.5 * hr * (1.0 + lax.erf(hr * (1.0 / math.sqrt(2.0))))
        coef.append(act.astype(BF16) * w)
    coef = jnp.concatenate(coef, axis=0)
    o_ref[...] += jnp.dot(vt_ref[...], coef, preferred_element_type=F32)


def _peer_dense(xt, u, vt, r1, e1, n0, e0, tb=512, rows=8):
    d, t = xt.shape
    e = u.shape[0]
    n_heads, n_keys, _ = r1.shape
    tb = min(tb, t)
    eb = rows * n_keys
    sel_j = pl.BlockSpec((n_heads, n_keys, tb), lambda i, j: (0, 0, i))
    sel_i = pl.BlockSpec((n_heads, rows, tb), lambda i, j: (0, j, i))
    return pl.pallas_call(
        functools.partial(_peer_dense_body, n_heads=n_heads, n_keys=n_keys, rows=rows),
        grid=(t // tb, e // eb),
        in_specs=[pl.BlockSpec((d, tb), lambda i, j: (0, i)),
                  pl.BlockSpec((eb, d), lambda i, j: (j, 0)),
                  pl.BlockSpec((d, eb), lambda i, j: (0, j)),
                  sel_j, sel_j, sel_i, sel_i],
        out_specs=pl.BlockSpec((d, tb), lambda i, j: (0, i)),
        out_shape=jax.ShapeDtypeStruct((d, t), F32),
        compiler_params=_cparams("parallel", "arbitrary"),
        name="peer_dense",
    )(xt, u, vt, r1, e1, n0, e0)


def _ln_t_body(x_ref, yt_ref, g_ref, b_ref, o_ref, ob_ref, *, alpha):
    x2 = _layer_norm(alpha * x_ref[...] + yt_ref[...].T, g_ref[...], b_ref[...])
    o_ref[...] = x2
    ob_ref[...] = x2.astype(BF16)


def _ln_t(x, yt, g, b, alpha, tm=256):
    t, d = x.shape
    tm = min(tm, t)
    tok = pl.BlockSpec((tm, d), lambda i: (i, 0))
    vec = pl.BlockSpec((1, d), lambda i: (0, 0))
    return pl.pallas_call(
        functools.partial(_ln_t_body, alpha=alpha),
        grid=(t // tm,),
        in_specs=[tok, pl.BlockSpec((d, tm), lambda i: (0, i)), vec, vec],
        out_specs=[tok, tok],
        out_shape=[jax.ShapeDtypeStruct((t, d), F32), jax.ShapeDtypeStruct((t, d), BF16)],
        compiler_params=_cparams("parallel"),
        name="peer_ln",
    )(x, yt, g.reshape(1, d), b.reshape(1, d))


def _peer(x, xt, w_q, sub_keys, u_tab, v_tab, g, b, alpha):
    t, d = x.shape
    n_heads, _, n_keys, dq = sub_keys.shape
    wq_h = w_q.astype(BF16).reshape(d, n_heads, 2 * dq).transpose(1, 0, 2)
    r1, e1, n0, e0 = _route(x, wq_h, sub_keys.astype(BF16))
    yt = _peer_dense(xt, u_tab.astype(BF16), v_tab.T.astype(BF16), r1, e1, n0, e0)
    return _ln_t(x, yt, g, b, alpha)


def _sb_attn_body(q_ref, k_ref, v_ref, o_ref, *, tq, scale):
    qi = pl.program_id(2)
    q = q_ref[...]
    row = lax.broadcasted_iota(jnp.int32, (tq, tq), 0)
    col = lax.broadcasted_iota(jnp.int32, (tq, tq), 1)
    later = jnp.where(row > col, 1.0, 0.0).astype(BF16)
    t_idx = qi * tq + row

    def step(n, carry):
        acc, run = carry
        kb = qi - n
        start = pl.multiple_of(kb * tq, tq)
        kblk = k_ref[pl.ds(start, tq), :]
        vblk = v_ref[pl.ds(start, tq), :]
        z = lax.dot_general(q, kblk, (((1,), (1,)), ((), ())), preferred_element_type=F32) * scale
        mask = (kb * tq + col) < t_idx
        log_b = jnp.minimum(z, 0.0) - jnp.log1p(jnp.exp(-jnp.abs(z)))
        log_1mb = jnp.where(mask, log_b - z, 0.0)
        hi = log_1mb.astype(BF16)
        lo = (log_1mb - hi.astype(F32)).astype(BF16)
        rest = (jnp.dot(hi, later, preferred_element_type=F32)
                + jnp.dot(lo, later, preferred_element_type=F32))
        att = jnp.where(mask, jnp.exp(log_b + rest + run), 0.0)
        acc = acc + jnp.dot(att.astype(BF16), vblk, preferred_element_type=F32)
        run = run + rest[:, 0:1] + log_1mb[:, 0:1]
        return acc, run

    acc0 = jnp.zeros((tq, q.shape[1]), F32)
    run0 = jnp.zeros((tq, 1), F32)
    acc, _ = lax.fori_loop(0, qi + 1, step, (acc0, run0))
    o_ref[...] = acc.astype(o_ref.dtype)


def _sb_attention(q3, kv3, n_heads, tq=256):
    b, s, d = q3.shape
    dh = d // n_heads
    tq = min(tq, s)
    return pl.pallas_call(
        functools.partial(_sb_attn_body, tq=tq, scale=dh ** -0.5),
        grid=(b, n_heads, s // tq),
        in_specs=[pl.BlockSpec((None, tq, dh), lambda bi, h, i: (bi, i, h)),
                  pl.BlockSpec((None, s, dh), lambda bi, h, i: (bi, 0, h)),
                  pl.BlockSpec((None, s, dh), lambda bi, h, i: (bi, 0, h + n_heads))],
        out_specs=pl.BlockSpec((None, tq, dh), lambda bi, h, i: (bi, i, h)),
        out_shape=jax.ShapeDtypeStruct((b, s, d), BF16),
        compiler_params=_cparams("parallel", "parallel", "arbitrary"),
        name="sb_attn",
    )(q3, kv3, kv3)


def kernel(x, conv_w_in, conv_b_in, conv_w_dw, conv_b_dw, conv_ln_g, conv_ln_b, conv_w_out, conv_b_out,
           attn_w_kv, attn_w_q, attn_w_o, peer_w_q, peer_sub_keys, peer_u, peer_v, ln_g, ln_b):
    bsz, seq, d = x.shape
    t = bsz * seq
    depth = peer_w_q.shape[0]
    n_a = conv_w_in.shape[0]
    alpha = (2.0 * depth) ** 0.25
    n_sb_heads = d // SB_HEAD_DIM

    xf = x.reshape(t, d)
    kv3 = None
    for i in range(depth):
        if i < n_a:
            g = _glu(xf, conv_w_in[i].astype(BF16), conv_b_in[i])
            c = _dwconv(g.reshape(bsz, seq, d), conv_w_dw[i].reshape(-1, d), conv_b_dw[i]).reshape(t, d)
            x1, x1t = _proj_ln(c, xf, conv_w_out[i].astype(BF16), conv_b_out[i], ln_g[i, 0], ln_b[i, 0], alpha,
                               pre=(conv_ln_g[i], conv_ln_b[i]))
        else:
            j = i - n_a
            q = _matmul(xb, attn_w_q[j].astype(BF16), BF16, name="attn_q")
            o = _sb_attention(q.reshape(bsz, seq, d), kv3, n_sb_heads).reshape(t, d)
            x1, x1t = _proj_ln(o, xf, attn_w_o[j].astype(BF16), jnp.zeros((d,), F32), ln_g[i, 0], ln_b[i, 0], alpha)
        xf, xb = _peer(x1, x1t, peer_w_q[i], peer_sub_keys[i], peer_u[i], peer_v[i], ln_g[i, 1], ln_b[i, 1], alpha)
        if i == n_a - 1:
            kv3 = _matmul(xb, attn_w_kv.astype(BF16), BF16, name="attn_kv").reshape(bsz, seq, 2 * d)
    return xf.reshape(bsz, seq, d)
```

```python
import functools
import math

import jax
import jax.numpy as jnp
from jax import lax
from jax.experimental import pallas as pl
from jax.experimental.pallas import tpu as pltpu

F32 = jnp.float32
BF16 = jnp.bfloat16

LN_EPS = 1e-5
LANES = 128
SUBLANES = 8
CONV_HALO = 32
PEER_TOPK = 16
SB_HEAD_DIM = 128
SB_LOG_CUTOFF = -104.0
VMEM_LIMIT_BYTES = 56 * 1024 * 1024


def _cparams(*sem):
    return pltpu.CompilerParams(dimension_semantics=sem, vmem_limit_bytes=VMEM_LIMIT_BYTES)


def _layer_norm(v, g, b):
    mu = jnp.mean(v, axis=-1, keepdims=True)
    c = v - mu
    var = jnp.mean(c * c, axis=-1, keepdims=True)
    return c * lax.rsqrt(var + LN_EPS) * g + b


def _mm_body(x_ref, w_ref, o_ref):
    o_ref[...] = jnp.dot(x_ref[...], w_ref[...], preferred_element_type=F32).astype(o_ref.dtype)


def _matmul(x, w, out_dtype, tm=1024, tn=1024, name="matmul"):
    m, k = x.shape
    n = w.shape[1]
    tm, tn = min(tm, m), min(tn, n)
    return pl.pallas_call(
        _mm_body,
        grid=(m // tm, n // tn),
        in_specs=[pl.BlockSpec((tm, k), lambda i, j: (i, 0)),
                  pl.BlockSpec((k, tn), lambda i, j: (0, j))],
        out_specs=pl.BlockSpec((tm, tn), lambda i, j: (i, j)),
        out_shape=jax.ShapeDtypeStruct((m, n), out_dtype),
        compiler_params=_cparams("parallel", "parallel"),
        name=name,
    )(x, w)


def _matmul_heads(x, w, width, tm=1024, name="matmul_heads"):
    m, k = x.shape
    n_heads = w.shape[1] // width
    tm = min(tm, m)
    return pl.pallas_call(
        _mm_body,
        grid=(m // tm, n_heads),
        in_specs=[pl.BlockSpec((tm, k), lambda i, j: (i, 0)),
                  pl.BlockSpec((k, width), lambda i, j: (0, j))],
        out_specs=pl.BlockSpec((None, tm, width), lambda i, j: (j, i, 0)),
        out_shape=jax.ShapeDtypeStruct((n_heads, m, width), BF16),
        compiler_params=_cparams("parallel", "parallel"),
        name=name,
    )(x, w)


def _glu_body(x_ref, wa_ref, wg_ref, ba_ref, bg_ref, o_ref):
    xb = x_ref[...].astype(BF16)
    a = jnp.dot(xb, wa_ref[...], preferred_element_type=F32) + ba_ref[...]
    gate = jnp.dot(xb, wg_ref[...], preferred_element_type=F32) + bg_ref[...]
    o_ref[...] = a * jax.nn.sigmoid(gate)


def _glu(x, w_in, b_in, tm=1024, tn=512):
    t, d = x.shape
    tm, tn = min(tm, t), min(tn, d)
    nb = d // tn
    b2 = b_in.reshape(1, 2 * d)
    return pl.pallas_call(
        _glu_body,
        grid=(t // tm, nb),
        in_specs=[pl.BlockSpec((tm, d), lambda i, j: (i, 0)),
                  pl.BlockSpec((d, tn), lambda i, j: (0, j)),
                  pl.BlockSpec((d, tn), lambda i, j: (0, j + nb)),
                  pl.BlockSpec((1, tn), lambda i, j: (0, j)),
                  pl.BlockSpec((1, tn), lambda i, j: (0, j + nb))],
        out_specs=pl.BlockSpec((tm, tn), lambda i, j: (i, j)),
        out_shape=jax.ShapeDtypeStruct((t, d), F32),
        compiler_params=_cparams("parallel", "parallel"),
        name="conv_glu",
    )(x, w_in, w_in, b2, b2)


def _dwconv_body(halo_ref, cur_ref, w_ref, b_ref, o_ref, buf_ref, shift_ref, *, width, ts):
    i = pl.program_id(1)
    halo = halo_ref[...]
    buf_ref[0:CONV_HALO, :] = jnp.where(i > 0, halo, jnp.zeros_like(halo))
    buf_ref[CONV_HALO:CONV_HALO + ts, :] = cur_ref[...]
    acc = jnp.broadcast_to(b_ref[...], o_ref.shape)
    first = CONV_HALO - (width - 1)
    for r in range(min(SUBLANES, width)):
        taps = range(r, width, SUBLANES)
        span = ts + (len(taps) - 1) * SUBLANES
        shift_ref[r, 0:span, :] = buf_ref[first + r:first + r + span, :]
        for a, k in enumerate(taps):
            acc = acc + w_ref[k:k + 1, :] * shift_ref[r, a * SUBLANES:a * SUBLANES + ts, :]
    o_ref[...] = acc


def _dwconv(g3, w_dw, b_dw, ts=512, td=256):
    b, s, d = g3.shape
    width = w_dw.shape[0]
    assert width <= CONV_HALO
    ts, td = min(ts, s), min(td, d)
    per = ts // CONV_HALO
    return pl.pallas_call(
        functools.partial(_dwconv_body, width=width, ts=ts),
        grid=(b, s // ts, d // td),
        in_specs=[pl.BlockSpec((None, CONV_HALO, td), lambda bi, i, j: (bi, jnp.maximum(i * per - 1, 0), j)),
                  pl.BlockSpec((None, ts, td), lambda bi, i, j: (bi, i, j)),
                  pl.BlockSpec((width, td), lambda bi, i, j: (0, j)),
                  pl.BlockSpec((1, td), lambda bi, i, j: (0, j))],
        out_specs=pl.BlockSpec((None, ts, td), lambda bi, i, j: (bi, i, j)),
        out_shape=jax.ShapeDtypeStruct((b, s, d), F32),
        scratch_shapes=[pltpu.VMEM((CONV_HALO + ts, td), F32),
                        pltpu.VMEM((SUBLANES, ts + CONV_HALO - SUBLANES, td), F32)],
        compiler_params=_cparams("parallel", "parallel", "parallel"),
        name="conv_dw",
    )(g3, g3, w_dw, b_dw.reshape(1, d))


def _proj_ln_body(*refs, alpha, pre_ln):
    if pre_ln:
        h_ref, x_ref, w_ref, pg_ref, pb_ref, bias_ref, g_ref, b_ref, o_ref, ob_ref, ot_ref = refs
        h = _layer_norm(h_ref[...], pg_ref[...], pb_ref[...])
        h = (h * jax.nn.sigmoid(h)).astype(BF16)
    else:
        h_ref, x_ref, w_ref, bias_ref, g_ref, b_ref, o_ref, ob_ref, ot_ref = refs
        h = h_ref[...]
    y = jnp.dot(h, w_ref[...], preferred_element_type=F32) + bias_ref[...]
    x1 = _layer_norm(alpha * x_ref[...] + y, g_ref[...], b_ref[...])
    o_ref[...] = x1
    ob_ref[...] = x1.astype(BF16)
    ot_ref[...] = x1.T.astype(BF16)


def _proj_ln(h, x, w, bias, g, b, alpha, pre=None, tm=256):
    t, d = x.shape
    tm = min(tm, t)
    row = lambda v: v.reshape(1, d)
    tok = pl.BlockSpec((tm, d), lambda i: (i, 0))
    vec = pl.BlockSpec((1, d), lambda i: (0, 0))
    mat = pl.BlockSpec((d, d), lambda i: (0, 0))
    if pre is not None:
        ins = [h, x, w, row(pre[0]), row(pre[1]), row(bias), row(g), row(b)]
        specs = [tok, tok, mat, vec, vec, vec, vec, vec]
    else:
        ins = [h, x, w, row(bias), row(g), row(b)]
        specs = [tok, tok, mat, vec, vec, vec]
    return pl.pallas_call(
        functools.partial(_proj_ln_body, alpha=alpha, pre_ln=pre is not None),
        grid=(t // tm,),
        in_specs=specs,
        out_specs=[tok, tok, pl.BlockSpec((d, tm), lambda i: (0, i))],
        out_shape=[jax.ShapeDtypeStruct((t, d), F32), jax.ShapeDtypeStruct((t, d), BF16),
                   jax.ShapeDtypeStruct((d, t), BF16)],
        compiler_params=_cparams("parallel"),
        name="proj_ln",
    )(*ins)


def _top_ranks(s, k, exact):
    n, tm = s.shape
    krow = lax.broadcasted_iota(jnp.int32, (k, tm), 0)
    rank = jnp.full((n, tm), float(k), F32)
    vals = jnp.zeros((k, tm), F32)
    if exact:
        row = lax.broadcasted_iota(jnp.int32, (n, tm), 0).astype(F32)
    for p in range(k):
        m = jnp.max(s, axis=0, keepdims=True)
        hit = s == m
        if exact:
            first = jnp.min(jnp.where(hit, row, float(n)), axis=0, keepdims=True)
            hit = row == first
        rank = jnp.where(hit, float(p), rank)
        s = jnp.where(hit, -jnp.inf, s)
        vals = jnp.where(krow == p, m, vals)
    ranked = jnp.sum(jnp.where(rank < float(k), 1.0, 0.0), axis=0, keepdims=True)
    return vals, rank, ranked


def _pair_counts(a, b, k, exact):
    tm = a.shape[1]
    half = k // 2
    p_full = lax.broadcasted_iota(jnp.int32, (k, tm), 0).astype(F32)
    p_half = lax.broadcasted_iota(jnp.int32, (half, tm), 0).astype(F32)
    cand = [a + b[0:1]]
    flat = [p_full * k]
    for q in range(1, k):
        lim = k // (q + 1)
        cand.append(jnp.where(p_half < lim, a[0:half] + b[q:q + 1], -jnp.inf))
        flat.append(p_half * k + q)
    cand = jnp.concatenate(cand, axis=0)
    flat = jnp.concatenate(flat, axis=0) if exact else None
    picked = jnp.zeros(cand.shape, F32)
    z = jnp.zeros((1, tm), F32)
    best = None
    for r in range(k):
        m = jnp.max(cand, axis=0, keepdims=True)
        best = m if r == 0 else best
        hit = cand == m
        if exact:
            first = jnp.min(jnp.where(hit, flat, float(k * k)), axis=0, keepdims=True)
            hit = flat == first
        picked = jnp.where(hit, 1.0, picked)
        cand = jnp.where(hit, -jnp.inf, cand)
        z = z + jnp.exp(m - best)
    count_lo = picked[0:half]
    for q in range(1, k):
        count_lo = count_lo + picked[k + (q - 1) * half:k + q * half]
    count = jnp.concatenate([count_lo, picked[half:k]], axis=0)
    n_picked = jnp.sum(count, axis=0, keepdims=True)
    return count, z, n_picked


def _route_tile(s0, s1, k, exact):
    a, rank0, ranked0 = _top_ranks(s0, k, exact)
    b, rank1, ranked1 = _top_ranks(s1, k, exact)
    count, z, n_picked = _pair_counts(a, b, k, exact)
    n0 = jnp.zeros_like(s0)
    for p in range(k):
        n0 = jnp.where(rank0 == float(p), count[p:p + 1], n0)
    e1 = jnp.exp(s1 - b[0:1]) / z
    e0 = jnp.exp(s0 - a[0:1])
    kf = float(k)
    clean = jnp.where((ranked0 == kf) & (ranked1 == kf) & (n_picked == kf), 0.0, 1.0)
    return rank1.astype(BF16), e1.astype(BF16), n0, e0, clean


def _route_body(q_ref, keys_ref, r1_ref, e1_ref, n0_ref, e0_ref, *, n_heads, k):
    n_tiles = r1_ref.shape[2] // LANES
    dq = keys_ref.shape[3]
    dn = (((1,), (1,)), ((), ()))

    def tile(idx, carry):
        h = idx // n_tiles
        c = idx % n_tiles
        lanes = pl.ds(pl.multiple_of(c * LANES, LANES), LANES)
        q = q_ref[h, lanes, :]
        s0 = lax.dot_general(keys_ref[h, 0], q[:, :dq], dn, preferred_element_type=F32)
        s1 = lax.dot_general(keys_ref[h, 1], q[:, dq:], dn, preferred_element_type=F32)

        def emit(exact):
            r1, e1, n0, e0, bad = _route_tile(s0, s1, k, exact)
            r1_ref[h, :, lanes] = r1
            e1_ref[h, :, lanes] = e1
            n0_ref[h, :, lanes] = n0
            e0_ref[h, :, lanes] = e0
            return bad

        bad = emit(False)

        @pl.when(jnp.max(bad) > 0.0)
        def _():
            emit(True)

        return carry

    lax.fori_loop(0, n_heads * n_tiles, tile, 0)


def _route(q3, keys, tm=512):
    n_heads, t, _ = q3.shape
    n_keys, dq = keys.shape[2], keys.shape[3]
    tm = min(tm, t)
    sel = pl.BlockSpec((n_heads, n_keys, tm), lambda i: (0, 0, i))
    shape = (n_heads, n_keys, t)
    return pl.pallas_call(
        functools.partial(_route_body, n_heads=n_heads, k=PEER_TOPK),
        grid=(t // tm,),
        in_specs=[pl.BlockSpec((n_heads, tm, 2 * dq), lambda i: (0, i, 0)),
                  pl.BlockSpec((n_heads, 2, n_keys, dq), lambda i: (0, 0, 0, 0))],
        out_specs=[sel, sel, sel, sel],
        out_shape=[jax.ShapeDtypeStruct(shape, BF16), jax.ShapeDtypeStruct(shape, BF16),
                   jax.ShapeDtypeStruct(shape, F32), jax.ShapeDtypeStruct(shape, F32)],
        compiler_params=_cparams("parallel"),
        name="peer_route",
    )(q3, keys)


def _peer_dense_body(xt_ref, u_ref, vt_ref, r1_ref, e1_ref, n0_ref, e0_ref, o_ref, *, rows):
    j = pl.program_id(1)
    n_heads, n_keys, tb = r1_ref.shape

    @pl.when(j == 0)
    def _():
        o_ref[...] = jnp.zeros_like(o_ref)

    ht = jnp.dot(u_ref[...], xt_ref[...], preferred_element_type=F32)
    coef = []
    for r in range(rows):
        w = jnp.zeros((n_keys, tb), BF16)
        for h in range(n_heads):
            n_row = jnp.broadcast_to(n0_ref[h, r:r + 1, :], (n_keys, tb)).astype(BF16)
            e_row = jnp.broadcast_to(e0_ref[h, r:r + 1, :], (n_keys, tb)).astype(BF16)
            w = w + jnp.where(r1_ref[h] < n_row, e1_ref[h], jnp.zeros((), BF16)) * e_row
        hr = ht[r * n_keys:(r + 1) * n_keys]
        act = 0.5 * hr * (1.0 + lax.erf(hr * (1.0 / math.sqrt(2.0))))
        coef.append(act.astype(BF16) * w)
    coef = jnp.concatenate(coef, axis=0)
    o_ref[...] += jnp.dot(vt_ref[...], coef, preferred_element_type=F32)


def _peer_dense(xt, u, vt, r1, e1, n0, e0, tb=512, rows=8):
    d, t = xt.shape
    e = u.shape[0]
    n_heads, n_keys, _ = r1.shape
    tb = min(tb, t)
    eb = rows * n_keys
    sel_j = pl.BlockSpec((n_heads, n_keys, tb), lambda i, j: (0, 0, i))
    sel_i = pl.BlockSpec((n_heads, rows, tb), lambda i, j: (0, j, i))
    return pl.pallas_call(
        functools.partial(_peer_dense_body, rows=rows),
        grid=(t // tb, e // eb),
        in_specs=[pl.BlockSpec((d, tb), lambda i, j: (0, i)),
                  pl.BlockSpec((eb, d), lambda i, j: (j, 0)),
                  pl.BlockSpec((d, eb), lambda i, j: (0, j)),
                  sel_j, sel_j, sel_i, sel_i],
        out_specs=pl.BlockSpec((d, tb), lambda i, j: (0, i)),
        out_shape=jax.ShapeDtypeStruct((d, t), F32),
        compiler_params=_cparams("parallel", "arbitrary"),
        name="peer_dense",
    )(xt, u, vt, r1, e1, n0, e0)


def _ln_t_body(x_ref, yt_ref, g_ref, b_ref, o_ref, ob_ref, *, alpha):
    x2 = _layer_norm(alpha * x_ref[...] + yt_ref[...].T, g_ref[...], b_ref[...])
    o_ref[...] = x2
    ob_ref[...] = x2.astype(BF16)


def _ln_t(x, yt, g, b, alpha, tm=256):
    t, d = x.shape
    tm = min(tm, t)
    tok = pl.BlockSpec((tm, d), lambda i: (i, 0))
    vec = pl.BlockSpec((1, d), lambda i: (0, 0))
    return pl.pallas_call(
        functools.partial(_ln_t_body, alpha=alpha),
        grid=(t // tm,),
        in_specs=[tok, pl.BlockSpec((d, tm), lambda i: (0, i)), vec, vec],
        out_specs=[tok, tok],
        out_shape=[jax.ShapeDtypeStruct((t, d), F32), jax.ShapeDtypeStruct((t, d), BF16)],
        compiler_params=_cparams("parallel"),
        name="peer_ln",
    )(x, yt, g.reshape(1, d), b.reshape(1, d))


def _peer(x, xb, xt, w_q, sub_keys, u_tab, v_tab, g, b, alpha):
    dq = sub_keys.shape[3]
    q3 = _matmul_heads(xb, w_q.astype(BF16), 2 * dq, name="peer_q")
    r1, e1, n0, e0 = _route(q3, sub_keys.astype(BF16))
    yt = _peer_dense(xt, u_tab.astype(BF16), v_tab.T.astype(BF16), r1, e1, n0, e0)
    return _ln_t(x, yt, g, b, alpha)


def _sb_attn_body(q_ref, k_ref, v_ref, o_ref, acc_ref, run_ref, *, tq, dh, hp, scale):
    qi = pl.program_id(2)
    row = lax.broadcasted_iota(jnp.int32, (tq, tq), 0)
    col = lax.broadcasted_iota(jnp.int32, (tq, tq), 1)
    later = jnp.where(row > col, 1.0, 0.0).astype(BF16)
    causal = col < row

    def block(a, kb, mask):
        start = pl.multiple_of(kb * tq, tq)
        lanes = slice(a * dh, (a + 1) * dh)
        q = q_ref[:, lanes]
        kblk = k_ref[pl.ds(start, tq), lanes]
        vblk = v_ref[pl.ds(start, tq), lanes]
        run = run_ref[a]
        z = lax.dot_general(q, kblk, (((1,), (1,)), ((), ())), preferred_element_type=F32) * scale
        log_b = jnp.minimum(z, 0.0) - jnp.log1p(jnp.exp(-jnp.abs(z)))
        log_1mb = log_b - z
        if mask is not None:
            log_1mb = jnp.where(mask, log_1mb, 0.0)
        hi = log_1mb.astype(BF16)
        lo = (log_1mb - hi.astype(F32)).astype(BF16)
        rest = (jnp.dot(hi, later, preferred_element_type=F32)
                + jnp.dot(lo, later, preferred_element_type=F32))
        att = jnp.exp(log_b + rest + run)
        if mask is not None:
            att = jnp.where(mask, att, 0.0)
        acc_ref[a] += jnp.dot(att.astype(BF16), vblk, preferred_element_type=F32)
        run = run + rest[:, 0:1] + log_1mb[:, 0:1]
        run_ref[a] = run
        return jnp.max(run)

    acc_ref[...] = jnp.zeros_like(acc_ref)
    run_ref[...] = jnp.zeros_like(run_ref)
    top = block(0, qi, causal)
    for a in range(1, hp):
        top = jnp.maximum(top, block(a, qi, causal))

    def cond(c):
        kb, top = c
        return (kb >= 0) & (top > SB_LOG_CUTOFF)

    def body(c):
        kb, _ = c
        top = block(0, kb, None)
        for a in range(1, hp):
            top = jnp.maximum(top, block(a, kb, None))
        return kb - 1, top

    lax.while_loop(cond, body, (qi - 1, top))
    for a in range(hp):
        o_ref[:, a * dh:(a + 1) * dh] = acc_ref[a].astype(o_ref.dtype)


def _sb_attention(q3, kv3, n_heads, tq=256, hp=2):
    b, s, d = q3.shape
    dh = d // n_heads
    tq = min(tq, s)
    groups = n_heads // hp
    return pl.pallas_call(
        functools.partial(_sb_attn_body, tq=tq, dh=dh, hp=hp, scale=dh ** -0.5),
        grid=(b, groups, s // tq),
        in_specs=[pl.BlockSpec((None, tq, hp * dh), lambda bi, h, i: (bi, i, h)),
                  pl.BlockSpec((None, s, hp * dh), lambda bi, h, i: (bi, 0, h)),
                  pl.BlockSpec((None, s, hp * dh), lambda bi, h, i: (bi, 0, h + groups))],
        out_specs=pl.BlockSpec((None, tq, hp * dh), lambda bi, h, i: (bi, i, h)),
        out_shape=jax.ShapeDtypeStruct((b, s, d), BF16),
        scratch_shapes=[pltpu.VMEM((hp, tq, dh), F32), pltpu.VMEM((hp, tq, 1), F32)],
        compiler_params=_cparams("parallel", "parallel", "arbitrary"),
        name="sb_attn",
    )(q3, kv3, kv3)


def kernel(x, conv_w_in, conv_b_in, conv_w_dw, conv_b_dw, conv_ln_g, conv_ln_b, conv_w_out, conv_b_out,
           attn_w_kv, attn_w_q, attn_w_o, peer_w_q, peer_sub_keys, peer_u, peer_v, ln_g, ln_b):
    bsz, seq, d = x.shape
    t = bsz * seq
    depth = peer_w_q.shape[0]
    n_a = conv_w_in.shape[0]
    alpha = (2.0 * depth) ** 0.25
    n_sb_heads = d // SB_HEAD_DIM

    xf = x.reshape(t, d)
    kv3 = None
    for i in range(depth):
        if i < n_a:
            g = _glu(xf, conv_w_in[i].astype(BF16), conv_b_in[i])
            c = _dwconv(g.reshape(bsz, seq, d), conv_w_dw[i].reshape(-1, d), conv_b_dw[i]).reshape(t, d)
            x1, x1b, x1t = _proj_ln(c, xf, conv_w_out[i].astype(BF16), conv_b_out[i], ln_g[i, 0], ln_b[i, 0],
                                    alpha, pre=(conv_ln_g[i], conv_ln_b[i]))
        else:
            j = i - n_a
            q = _matmul(xb, attn_w_q[j].astype(BF16), BF16, name="attn_q")
            o = _sb_attention(q.reshape(bsz, seq, d), kv3, n_sb_heads).reshape(t, d)
            x1, x1b, x1t = _proj_ln(o, xf, attn_w_o[j].astype(BF16), jnp.zeros((d,), F32), ln_g[i, 0],
                                    ln_b[i, 0], alpha)
        xf, xb = _peer(x1, x1b, x1t, peer_w_q[i], peer_sub_keys[i], peer_u[i], peer_v[i],
                       ln_g[i, 1], ln_b[i, 1], alpha)
        if i == n_a - 1:
            kv3 = _matmul(xb, attn_w_kv.astype(BF16), BF16, name="attn_kv").reshape(bsz, seq, 2 * d)
    return xf.reshape(bsz, seq, d)
```

```python
import functools
import math

import jax
import jax.numpy as jnp
from jax import lax
from jax.experimental import pallas as pl
from jax.experimental.pallas import tpu as pltpu

F32 = jnp.float32
BF16 = jnp.bfloat16

LN_EPS = 1e-5
LANES = 128
SUBLANES = 8
CONV_HALO = 32
PEER_TOPK = 16
SB_HEAD_DIM = 128
SB_LOG_CUTOFF = -104.0
VMEM_LIMIT_BYTES = 56 * 1024 * 1024


def _cparams(*sem):
    return pltpu.CompilerParams(dimension_semantics=sem, vmem_limit_bytes=VMEM_LIMIT_BYTES)


def _layer_norm(v, g, b):
    mu = jnp.mean(v, axis=-1, keepdims=True)
    c = v - mu
    var = jnp.mean(c * c, axis=-1, keepdims=True)
    return c * lax.rsqrt(var + LN_EPS) * g + b


def _mm_body(x_ref, w_ref, o_ref):
    o_ref[...] = jnp.dot(x_ref[...], w_ref[...], preferred_element_type=F32).astype(o_ref.dtype)


def _matmul(x, w, out_dtype, tm=1024, tn=1024, name="matmul"):
    m, k = x.shape
    n = w.shape[1]
    tm, tn = min(tm, m), min(tn, n)
    return pl.pallas_call(
        _mm_body,
        grid=(m // tm, n // tn),
        in_specs=[pl.BlockSpec((tm, k), lambda i, j: (i, 0)),
                  pl.BlockSpec((k, tn), lambda i, j: (0, j))],
        out_specs=pl.BlockSpec((tm, tn), lambda i, j: (i, j)),
        out_shape=jax.ShapeDtypeStruct((m, n), out_dtype),
        compiler_params=_cparams("parallel", "parallel"),
        name=name,
    )(x, w)


def _matmul_heads(x, w, width, tm=1024, name="matmul_heads"):
    m, k = x.shape
    n_heads = w.shape[1] // width
    tm = min(tm, m)
    return pl.pallas_call(
        _mm_body,
        grid=(m // tm, n_heads),
        in_specs=[pl.BlockSpec((tm, k), lambda i, j: (i, 0)),
                  pl.BlockSpec((k, width), lambda i, j: (0, j))],
        out_specs=pl.BlockSpec((None, tm, width), lambda i, j: (j, i, 0)),
        out_shape=jax.ShapeDtypeStruct((n_heads, m, width), BF16),
        compiler_params=_cparams("parallel", "parallel"),
        name=name,
    )(x, w)


def _glu_body(x_ref, wa_ref, wg_ref, ba_ref, bg_ref, o_ref):
    xb = x_ref[...].astype(BF16)
    a = jnp.dot(xb, wa_ref[...], preferred_element_type=F32) + ba_ref[...]
    gate = jnp.dot(xb, wg_ref[...], preferred_element_type=F32) + bg_ref[...]
    o_ref[...] = a * jax.nn.sigmoid(gate)


def _glu(x, w_in, b_in, tm=1024, tn=512):
    t, d = x.shape
    tm, tn = min(tm, t), min(tn, d)
    nb = d // tn
    b2 = b_in.reshape(1, 2 * d)
    return pl.pallas_call(
        _glu_body,
        grid=(t // tm, nb),
        in_specs=[pl.BlockSpec((tm, d), lambda i, j: (i, 0)),
                  pl.BlockSpec((d, tn), lambda i, j: (0, j)),
                  pl.BlockSpec((d, tn), lambda i, j: (0, j + nb)),
                  pl.BlockSpec((1, tn), lambda i, j: (0, j)),
                  pl.BlockSpec((1, tn), lambda i, j: (0, j + nb))],
        out_specs=pl.BlockSpec((tm, tn), lambda i, j: (i, j)),
        out_shape=jax.ShapeDtypeStruct((t, d), F32),
        compiler_params=_cparams("parallel", "parallel"),
        name="conv_glu",
    )(x, w_in, w_in, b2, b2)


def _dwconv_body(halo_ref, cur_ref, w_ref, b_ref, o_ref, buf_ref, shift_ref, *, width, ts):
    i = pl.program_id(1)
    halo = halo_ref[...]
    buf_ref[0:CONV_HALO, :] = jnp.where(i > 0, halo, jnp.zeros_like(halo))
    buf_ref[CONV_HALO:CONV_HALO + ts, :] = cur_ref[...]
    acc = jnp.broadcast_to(b_ref[...], o_ref.shape)
    first = CONV_HALO - (width - 1)
    for r in range(min(SUBLANES, width)):
        taps = range(r, width, SUBLANES)
        span = ts + (len(taps) - 1) * SUBLANES
        shift_ref[r, 0:span, :] = buf_ref[first + r:first + r + span, :]
        for a, k in enumerate(taps):
            acc = acc + w_ref[k:k + 1, :] * shift_ref[r, a * SUBLANES:a * SUBLANES + ts, :]
    o_ref[...] = acc


def _dwconv(g3, w_dw, b_dw, ts=512, td=256):
    b, s, d = g3.shape
    width = w_dw.shape[0]
    assert width <= CONV_HALO
    ts, td = min(ts, s), min(td, d)
    per = ts // CONV_HALO
    return pl.pallas_call(
        functools.partial(_dwconv_body, width=width, ts=ts),
        grid=(b, s // ts, d // td),
        in_specs=[pl.BlockSpec((None, CONV_HALO, td), lambda bi, i, j: (bi, jnp.maximum(i * per - 1, 0), j)),
                  pl.BlockSpec((None, ts, td), lambda bi, i, j: (bi, i, j)),
                  pl.BlockSpec((width, td), lambda bi, i, j: (0, j)),
                  pl.BlockSpec((1, td), lambda bi, i, j: (0, j))],
        out_specs=pl.BlockSpec((None, ts, td), lambda bi, i, j: (bi, i, j)),
        out_shape=jax.ShapeDtypeStruct((b, s, d), F32),
        scratch_shapes=[pltpu.VMEM((CONV_HALO + ts, td), F32),
                        pltpu.VMEM((SUBLANES, ts + CONV_HALO - SUBLANES, td), F32)],
        compiler_params=_cparams("parallel", "parallel", "parallel"),
        name="conv_dw",
    )(g3, g3, w_dw, b_dw.reshape(1, d))


def _proj_ln_body(*refs, alpha, pre_ln):
    if pre_ln:
        h_ref, x_ref, w_ref, pg_ref, pb_ref, bias_ref, g_ref, b_ref, o_ref, ob_ref, ot_ref = refs
        h = _layer_norm(h_ref[...], pg_ref[...], pb_ref[...])
        h = (h * jax.nn.sigmoid(h)).astype(BF16)
    else:
        h_ref, x_ref, w_ref, bias_ref, g_ref, b_ref, o_ref, ob_ref, ot_ref = refs
        h = h_ref[...]
    y = jnp.dot(h, w_ref[...], preferred_element_type=F32) + bias_ref[...]
    x1 = _layer_norm(alpha * x_ref[...] + y, g_ref[...], b_ref[...])
    o_ref[...] = x1
    ob_ref[...] = x1.astype(BF16)
    ot_ref[...] = x1.T.astype(BF16)


def _proj_ln(h, x, w, bias, g, b, alpha, pre=None, tm=256):
    t, d = x.shape
    tm = min(tm, t)
    row = lambda v: v.reshape(1, d)
    tok = pl.BlockSpec((tm, d), lambda i: (i, 0))
    vec = pl.BlockSpec((1, d), lambda i: (0, 0))
    mat = pl.BlockSpec((d, d), lambda i: (0, 0))
    if pre is not None:
        ins = [h, x, w, row(pre[0]), row(pre[1]), row(bias), row(g), row(b)]
        specs = [tok, tok, mat, vec, vec, vec, vec, vec]
    else:
        ins = [h, x, w, row(bias), row(g), row(b)]
        specs = [tok, tok, mat, vec, vec, vec]
    return pl.pallas_call(
        functools.partial(_proj_ln_body, alpha=alpha, pre_ln=pre is not None),
        grid=(t // tm,),
        in_specs=specs,
        out_specs=[tok, tok, pl.BlockSpec((d, tm), lambda i: (0, i))],
        out_shape=[jax.ShapeDtypeStruct((t, d), F32), jax.ShapeDtypeStruct((t, d), BF16),
                   jax.ShapeDtypeStruct((d, t), BF16)],
        compiler_params=_cparams("parallel"),
        name="proj_ln",
    )(*ins)


def _top_ranks(s, k, exact):
    n, tm = s.shape
    krow = lax.broadcasted_iota(jnp.int32, (k, tm), 0)
    rank = jnp.full((n, tm), float(k), F32)
    vals = jnp.zeros((k, tm), F32)
    if exact:
        row = lax.broadcasted_iota(jnp.int32, (n, tm), 0).astype(F32)
    for p in range(k):
        m = jnp.max(s, axis=0, keepdims=True)
        hit = s == m
        if exact:
            first = jnp.min(jnp.where(hit, row, float(n)), axis=0, keepdims=True)
            hit = row == first
        rank = jnp.where(hit, float(p), rank)
        s = jnp.where(hit, -jnp.inf, s)
        vals = jnp.where(krow == p, m, vals)
    ranked = jnp.sum(jnp.where(rank < float(k), 1.0, 0.0), axis=0, keepdims=True)
    return vals, rank, ranked


def _pair_counts(a, b, k, exact):
    tm = a.shape[1]
    half = k // 2
    p_full = lax.broadcasted_iota(jnp.int32, (k, tm), 0).astype(F32)
    p_half = lax.broadcasted_iota(jnp.int32, (half, tm), 0).astype(F32)
    cand = [a + b[0:1]]
    flat = [p_full * k]
    for q in range(1, k):
        lim = k // (q + 1)
        cand.append(jnp.where(p_half < lim, a[0:half] + b[q:q + 1], -jnp.inf))
        flat.append(p_half * k + q)
    cand = jnp.concatenate(cand, axis=0)
    flat = jnp.concatenate(flat, axis=0) if exact else None
    picked = jnp.zeros(cand.shape, F32)
    z = jnp.zeros((1, tm), F32)
    best = None
    for r in range(k):
        m = jnp.max(cand, axis=0, keepdims=True)
        best = m if r == 0 else best
        hit = cand == m
        if exact:
            first = jnp.min(jnp.where(hit, flat, float(k * k)), axis=0, keepdims=True)
            hit = flat == first
        picked = jnp.where(hit, 1.0, picked)
        cand = jnp.where(hit, -jnp.inf, cand)
        z = z + jnp.exp(m - best)
    count_lo = picked[0:half]
    for q in range(1, k):
        count_lo = count_lo + picked[k + (q - 1) * half:k + q * half]
    count = jnp.concatenate([count_lo, picked[half:k]], axis=0)
    n_picked = jnp.sum(count, axis=0, keepdims=True)
    return count, z, n_picked


def _route_tile(s0, s1, k, exact):
    a, rank0, ranked0 = _top_ranks(s0, k, exact)
    b, rank1, ranked1 = _top_ranks(s1, k, exact)
    count, z, n_picked = _pair_counts(a, b, k, exact)
    n0 = jnp.zeros_like(s0)
    for p in range(k):
        n0 = jnp.where(rank0 == float(p), count[p:p + 1], n0)
    e1 = jnp.exp(s1 - b[0:1]) / z
    e0 = jnp.exp(s0 - a[0:1])
    kf = float(k)
    clean = jnp.where((ranked0 == kf) & (ranked1 == kf) & (n_picked == kf), 0.0, 1.0)
    return rank1.astype(BF16), e1.astype(BF16), n0, e0, clean


def _route_pair(q_ref, keys_ref, pair, n_tiles, k, outs):
    dq = keys_ref.shape[3]
    dn = (((1,), (1,)), ((), ()))
    h = pair // n_tiles
    lanes = pl.ds(pl.multiple_of((pair % n_tiles) * LANES, LANES), LANES)
    q = q_ref[h, lanes, :]
    s0 = lax.dot_general(keys_ref[h, 0], q[:, :dq], dn, preferred_element_type=F32)
    s1 = lax.dot_general(keys_ref[h, 1], q[:, dq:], dn, preferred_element_type=F32)

    def emit(exact):
        *vals, bad = _route_tile(s0, s1, k, exact)
        for ref, val in zip(outs, vals):
            ref[h, :, lanes] = val
        return jnp.max(bad)

    flag = emit(False)

    def redo():
        emit(True)

    return redo, flag


def _peer_dense_body(q0_ref, qn_ref, keys_ref, xt_ref, u_ref, vt_ref, o_ref, r1_s, e1_s, n0_s, e0_s,
                     *, rows, k, pairs_per_step):
    i = pl.program_id(0)
    j = pl.program_id(1)
    _, n_heads, n_keys, tb = r1_s.shape
    n_tiles = tb // LANES
    n_pairs = n_heads * n_tiles
    cur = i % 2
    slots = lambda s: [r1_s.at[s], e1_s.at[s], n0_s.at[s], e0_s.at[s]]

    @pl.when(j == 0)
    def _():
        o_ref[...] = jnp.zeros_like(o_ref)

    @pl.when((i == 0) & (j == 0))
    def _():
        def first(pair, carry):
            redo, flag = _route_pair(q0_ref, keys_ref, pair, n_tiles, k, slots(0))
            pl.when(flag > 0.0)(redo)
            return carry
        lax.fori_loop(0, n_pairs, first, 0)

    pending = [_route_pair(qn_ref, keys_ref, jnp.minimum(j * pairs_per_step + a, n_pairs - 1), n_tiles, k,
                           slots(1 - cur)) for a in range(pairs_per_step)]

    ht = jnp.dot(u_ref[...], xt_ref[...], preferred_element_type=F32)
    coef = []
    for r in range(rows):
        w = jnp.zeros((n_keys, tb), BF16)
        for h in range(n_heads):
            n_row = jnp.broadcast_to(n0_s[cur, h, pl.ds(j * rows + r, 1), :], (n_keys, tb)).astype(BF16)
            e_row = jnp.broadcast_to(e0_s[cur, h, pl.ds(j * rows + r, 1), :], (n_keys, tb)).astype(BF16)
            w = w + jnp.where(r1_s[cur, h] < n_row, e1_s[cur, h], jnp.zeros((), BF16)) * e_row
        hr = ht[r * n_keys:(r + 1) * n_keys]
        act = 0.5 * hr * (1.0 + lax.erf(hr * (1.0 / math.sqrt(2.0))))
        coef.append(act.astype(BF16) * w)
    coef = jnp.concatenate(coef, axis=0)
    o_ref[...] += jnp.dot(vt_ref[...], coef, preferred_element_type=F32)

    for redo, flag in pending:
        pl.when(flag > 0.0)(redo)


def _peer_dense(q3, keys, xt, u, vt, tb=512, rows=8):
    d, t = xt.shape
    e = u.shape[0]
    n_heads, _, n_keys, dq = keys.shape
    tb = min(tb, t)
    eb = rows * n_keys
    assert rows == SUBLANES and tb % LANES == 0
    n_tok, n_exp = t // tb, e // eb
    pairs_per_step = -(-(n_heads * (tb // LANES)) // n_exp)
    slot_bf16 = pltpu.VMEM((2, n_heads, n_keys, tb), BF16)
    slot_f32 = pltpu.VMEM((2, n_heads, n_keys, tb), F32)
    return pl.pallas_call(
        functools.partial(_peer_dense_body, rows=rows, k=PEER_TOPK, pairs_per_step=pairs_per_step),
        grid=(n_tok, n_exp),
        in_specs=[pl.BlockSpec((n_heads, tb, 2 * dq), lambda i, j: (0, 0, 0)),
                  pl.BlockSpec((n_heads, tb, 2 * dq), lambda i, j: (0, jnp.minimum(i + 1, n_tok - 1), 0)),
                  pl.BlockSpec((n_heads, 2, n_keys, dq), lambda i, j: (0, 0, 0, 0)),
                  pl.BlockSpec((d, tb), lambda i, j: (0, i)),
                  pl.BlockSpec((eb, d), lambda i, j: (j, 0)),
                  pl.BlockSpec((d, eb), lambda i, j: (0, j))],
        out_specs=pl.BlockSpec((d, tb), lambda i, j: (0, i)),
        out_shape=jax.ShapeDtypeStruct((d, t), F32),
        scratch_shapes=[slot_bf16, slot_bf16, slot_f32, slot_f32],
        compiler_params=_cparams("arbitrary", "arbitrary"),
        name="peer_dense",
    )(q3, q3, keys, xt, u, vt)


def _ln_t_body(x_ref, yt_ref, g_ref, b_ref, o_ref, ob_ref, *, alpha):
    x2 = _layer_norm(alpha * x_ref[...] + yt_ref[...].T, g_ref[...], b_ref[...])
    o_ref[...] = x2
    ob_ref[...] = x2.astype(BF16)


def _ln_t(x, yt, g, b, alpha, tm=256):
    t, d = x.shape
    tm = min(tm, t)
    tok = pl.BlockSpec((tm, d), lambda i: (i, 0))
    vec = pl.BlockSpec((1, d), lambda i: (0, 0))
    return pl.pallas_call(
        functools.partial(_ln_t_body, alpha=alpha),
        grid=(t // tm,),
        in_specs=[tok, pl.BlockSpec((d, tm), lambda i: (0, i)), vec, vec],
        out_specs=[tok, tok],
        out_shape=[jax.ShapeDtypeStruct((t, d), F32), jax.ShapeDtypeStruct((t, d), BF16)],
        compiler_params=_cparams("parallel"),
        name="peer_ln",
    )(x, yt, g.reshape(1, d), b.reshape(1, d))


def _peer(x, xb, xt, w_q, sub_keys, u_tab, v_tab, g, b, alpha):
    dq = sub_keys.shape[3]
    q3 = _matmul_heads(xb, w_q.astype(BF16), 2 * dq, name="peer_q")
    yt = _peer_dense(q3, sub_keys.astype(BF16), xt, u_tab.astype(BF16), v_tab.T.astype(BF16))
    return _ln_t(x, yt, g, b, alpha)


def _sb_attn_body(q_ref, k_ref, v_ref, o_ref, acc_ref, run_ref, *, tq, dh, hp, scale):
    qi = pl.program_id(2)
    row = lax.broadcasted_iota(jnp.int32, (tq, tq), 0)
    col = lax.broadcasted_iota(jnp.int32, (tq, tq), 1)
    later = jnp.where(row > col, 1.0, 0.0).astype(BF16)
    causal = col < row
    before = col + jnp.where(qi > 0, 0, 2 * tq) < row + tq

    def block(a, kb, mask, run):
        start = pl.multiple_of(kb * tq, tq)
        lanes = slice(a * dh, (a + 1) * dh)
        kblk = k_ref[pl.ds(start, tq), lanes]
        vblk = v_ref[pl.ds(start, tq), lanes]
        z = lax.dot_general(q_ref[:, lanes], kblk, (((1,), (1,)), ((), ())), preferred_element_type=F32) * scale
        log_b = jnp.minimum(z, 0.0) - jnp.log1p(jnp.exp(-jnp.abs(z)))
        log_1mb = log_b - z
        if mask is not None:
            log_1mb = jnp.where(mask, log_1mb, 0.0)
        hi = log_1mb.astype(BF16)
        lo = (log_1mb - hi.astype(F32)).astype(BF16)
        rest = (jnp.dot(hi, later, preferred_element_type=F32)
                + jnp.dot(lo, later, preferred_element_type=F32))
        att = jnp.exp(log_b + rest + run)
        if mask is not None:
            att = jnp.where(mask, att, 0.0)
        pv = jnp.dot(att.astype(BF16), vblk, preferred_element_type=F32)
        return pv, run + rest[:, 0:1] + log_1mb[:, 0:1]

    top = None
    for a in range(hp):
        pv0, run = block(a, qi, causal, jnp.zeros((tq, 1), F32))
        pv1, run = block(a, jnp.maximum(qi - 1, 0), before, run)
        acc_ref[a] = pv0 + pv1
        run_ref[a] = run
        top = jnp.max(run) if top is None else jnp.maximum(top, jnp.max(run))

    def cond(c):
        kb, top = c
        return (kb >= 0) & (top > SB_LOG_CUTOFF)

    def body(c):
        kb, _ = c
        top = None
        for a in range(hp):
            pv, run = block(a, kb, None, run_ref[a])
            acc_ref[a] += pv
            run_ref[a] = run
            top = jnp.max(run) if top is None else jnp.maximum(top, jnp.max(run))
        return kb - 1, top

    lax.while_loop(cond, body, (qi - 2, top))
    for a in range(hp):
        o_ref[:, a * dh:(a + 1) * dh] = acc_ref[a].astype(o_ref.dtype)


def _sb_attention(q3, kv3, n_heads, tq=256, hp=4):
    b, s, d = q3.shape
    dh = d // n_heads
    tq = min(tq, s)
    hp = min(hp, n_heads)
    groups = n_heads // hp
    return pl.pallas_call(
        functools.partial(_sb_attn_body, tq=tq, dh=dh, hp=hp, scale=dh ** -0.5),
        grid=(b, groups, s // tq),
        in_specs=[pl.BlockSpec((None, tq, hp * dh), lambda bi, h, i: (bi, i, h)),
                  pl.BlockSpec((None, s, hp * dh), lambda bi, h, i: (bi, 0, h)),
                  pl.BlockSpec((None, s, hp * dh), lambda bi, h, i: (bi, 0, h + groups))],
        out_specs=pl.BlockSpec((None, tq, hp * dh), lambda bi, h, i: (bi, i, h)),
        out_shape=jax.ShapeDtypeStruct((b, s, d), BF16),
        scratch_shapes=[pltpu.VMEM((hp, tq, dh), F32), pltpu.VMEM((hp, tq, 1), F32)],
        compiler_params=_cparams("parallel", "parallel", "arbitrary"),
        name="sb_attn",
    )(q3, kv3, kv3)


def kernel(x, conv_w_in, conv_b_in, conv_w_dw, conv_b_dw, conv_ln_g, conv_ln_b, conv_w_out, conv_b_out,
           attn_w_kv, attn_w_q, attn_w_o, peer_w_q, peer_sub_keys, peer_u, peer_v, ln_g, ln_b):
    bsz, seq, d = x.shape
    t = bsz * seq
    depth = peer_w_q.shape[0]
    n_a = conv_w_in.shape[0]
    alpha = (2.0 * depth) ** 0.25
    n_sb_heads = d // SB_HEAD_DIM

    xf = x.reshape(t, d)
    kv3 = None
    for i in range(depth):
        if i < n_a:
            g = _glu(xf, conv_w_in[i].astype(BF16), conv_b_in[i])
            c = _dwconv(g.reshape(bsz, seq, d), conv_w_dw[i].reshape(-1, d), conv_b_dw[i]).reshape(t, d)
            x1, x1b, x1t = _proj_ln(c, xf, conv_w_out[i].astype(BF16), conv_b_out[i], ln_g[i, 0], ln_b[i, 0],
                                    alpha, pre=(conv_ln_g[i], conv_ln_b[i]))
        else:
            j = i - n_a
            q = _matmul(xb, attn_w_q[j].astype(BF16), BF16, name="attn_q")
            o = _sb_attention(q.reshape(bsz, seq, d), kv3, n_sb_heads).reshape(t, d)
            x1, x1b, x1t = _proj_ln(o, xf, attn_w_o[j].astype(BF16), jnp.zeros((d,), F32), ln_g[i, 0],
                                    ln_b[i, 0], alpha)
        xf, xb = _peer(x1, x1b, x1t, peer_w_q[i], peer_sub_keys[i], peer_u[i], peer_v[i],
                       ln_g[i, 1], ln_b[i, 1], alpha)
        if i == n_a - 1:
            kv3 = _matmul(xb, attn_w_kv.astype(BF16), BF16, name="attn_kv").reshape(bsz, seq, 2 * d)
    return xf.reshape(bsz, seq, d)
```

```python
import functools
import math

import jax
import jax.numpy as jnp
from jax import lax
from jax.experimental import pallas as pl
from jax.experimental.pallas import tpu as pltpu

F32 = jnp.float32
BF16 = jnp.bfloat16

LN_EPS = 1e-5
LANES = 128
SUBLANES = 8
CONV_HALO = 32
PEER_TOPK = 16
SB_HEAD_DIM = 128
SB_LOG_CUTOFF = -104.0
VMEM_LIMIT_BYTES = 56 * 1024 * 1024


def _cparams(*sem):
    return pltpu.CompilerParams(dimension_semantics=sem, vmem_limit_bytes=VMEM_LIMIT_BYTES)


def _layer_norm(v, g, b):
    mu = jnp.mean(v, axis=-1, keepdims=True)
    c = v - mu
    var = jnp.mean(c * c, axis=-1, keepdims=True)
    return c * lax.rsqrt(var + LN_EPS) * g + b


def _mm_body(x_ref, w_ref, o_ref):
    o_ref[...] = jnp.dot(x_ref[...], w_ref[...], preferred_element_type=F32).astype(o_ref.dtype)


def _matmul(x, w, out_dtype, tm=1024, tn=1024, name="matmul"):
    m, k = x.shape
    n = w.shape[1]
    tm, tn = min(tm, m), min(tn, n)
    return pl.pallas_call(
        _mm_body,
        grid=(m // tm, n // tn),
        in_specs=[pl.BlockSpec((tm, k), lambda i, j: (i, 0)),
                  pl.BlockSpec((k, tn), lambda i, j: (0, j))],
        out_specs=pl.BlockSpec((tm, tn), lambda i, j: (i, j)),
        out_shape=jax.ShapeDtypeStruct((m, n), out_dtype),
        compiler_params=_cparams("parallel", "parallel"),
        name=name,
    )(x, w)


def _matmul_heads(x, w, width, tm=1024, name="matmul_heads"):
    m, k = x.shape
    n_heads = w.shape[1] // width
    tm = min(tm, m)
    return pl.pallas_call(
        _mm_body,
        grid=(m // tm, n_heads),
        in_specs=[pl.BlockSpec((tm, k), lambda i, j: (i, 0)),
                  pl.BlockSpec((k, width), lambda i, j: (0, j))],
        out_specs=pl.BlockSpec((None, tm, width), lambda i, j: (j, i, 0)),
        out_shape=jax.ShapeDtypeStruct((n_heads, m, width), BF16),
        compiler_params=_cparams("parallel", "parallel"),
        name=name,
    )(x, w)


def _glu_body(x_ref, wa_ref, wg_ref, ba_ref, bg_ref, o_ref):
    xb = x_ref[...].astype(BF16)
    a = jnp.dot(xb, wa_ref[...], preferred_element_type=F32) + ba_ref[...]
    gate = jnp.dot(xb, wg_ref[...], preferred_element_type=F32) + bg_ref[...]
    o_ref[...] = a * jax.nn.sigmoid(gate)


def _glu(x, w_in, b_in, tm=1024, tn=512):
    t, d = x.shape
    tm, tn = min(tm, t), min(tn, d)
    nb = d // tn
    b2 = b_in.reshape(1, 2 * d)
    return pl.pallas_call(
        _glu_body,
        grid=(t // tm, nb),
        in_specs=[pl.BlockSpec((tm, d), lambda i, j: (i, 0)),
                  pl.BlockSpec((d, tn), lambda i, j: (0, j)),
                  pl.BlockSpec((d, tn), lambda i, j: (0, j + nb)),
                  pl.BlockSpec((1, tn), lambda i, j: (0, j)),
                  pl.BlockSpec((1, tn), lambda i, j: (0, j + nb))],
        out_specs=pl.BlockSpec((tm, tn), lambda i, j: (i, j)),
        out_shape=jax.ShapeDtypeStruct((t, d), F32),
        compiler_params=_cparams("parallel", "parallel"),
        name="conv_glu",
    )(x, w_in, w_in, b2, b2)


def _dwconv_body(halo_ref, cur_ref, w_ref, b_ref, o_ref, buf_ref, shift_ref, *, width, ts):
    i = pl.program_id(1)
    halo = halo_ref[...]
    buf_ref[0:CONV_HALO, :] = jnp.where(i > 0, halo, jnp.zeros_like(halo))
    buf_ref[CONV_HALO:CONV_HALO + ts, :] = cur_ref[...]
    acc = jnp.broadcast_to(b_ref[...], o_ref.shape)
    first = CONV_HALO - (width - 1)
    for r in range(min(SUBLANES, width)):
        taps = range(r, width, SUBLANES)
        span = ts + (len(taps) - 1) * SUBLANES
        shift_ref[r, 0:span, :] = buf_ref[first + r:first + r + span, :]
        for a, k in enumerate(taps):
            acc = acc + w_ref[k:k + 1, :] * shift_ref[r, a * SUBLANES:a * SUBLANES + ts, :]
    o_ref[...] = acc


def _dwconv(g3, w_dw, b_dw, ts=512, td=256):
    b, s, d = g3.shape
    width = w_dw.shape[0]
    assert width <= CONV_HALO
    ts, td = min(ts, s), min(td, d)
    per = ts // CONV_HALO
    return pl.pallas_call(
        functools.partial(_dwconv_body, width=width, ts=ts),
        grid=(b, s // ts, d // td),
        in_specs=[pl.BlockSpec((None, CONV_HALO, td), lambda bi, i, j: (bi, jnp.maximum(i * per - 1, 0), j)),
                  pl.BlockSpec((None, ts, td), lambda bi, i, j: (bi, i, j)),
                  pl.BlockSpec((width, td), lambda bi, i, j: (0, j)),
                  pl.BlockSpec((1, td), lambda bi, i, j: (0, j))],
        out_specs=pl.BlockSpec((None, ts, td), lambda bi, i, j: (bi, i, j)),
        out_shape=jax.ShapeDtypeStruct((b, s, d), F32),
        scratch_shapes=[pltpu.VMEM((CONV_HALO + ts, td), F32),
                        pltpu.VMEM((SUBLANES, ts + CONV_HALO - SUBLANES, td), F32)],
        compiler_params=_cparams("parallel", "parallel", "parallel"),
        name="conv_dw",
    )(g3, g3, w_dw, b_dw.reshape(1, d))


def _proj_ln_body(*refs, alpha, pre_ln):
    if pre_ln:
        h_ref, x_ref, w_ref, pg_ref, pb_ref, bias_ref, g_ref, b_ref, o_ref, ob_ref, ot_ref = refs
        h = _layer_norm(h_ref[...], pg_ref[...], pb_ref[...])
        h = (h * jax.nn.sigmoid(h)).astype(BF16)
    else:
        h_ref, x_ref, w_ref, bias_ref, g_ref, b_ref, o_ref, ob_ref, ot_ref = refs
        h = h_ref[...]
    y = jnp.dot(h, w_ref[...], preferred_element_type=F32) + bias_ref[...]
    x1 = _layer_norm(alpha * x_ref[...] + y, g_ref[...], b_ref[...])
    o_ref[...] = x1
    ob_ref[...] = x1.astype(BF16)
    ot_ref[...] = x1.T.astype(BF16)


def _proj_ln(h, x, w, bias, g, b, alpha, pre=None, tm=256):
    t, d = x.shape
    tm = min(tm, t)
    row = lambda v: v.reshape(1, d)
    tok = pl.BlockSpec((tm, d), lambda i: (i, 0))
    vec = pl.BlockSpec((1, d), lambda i: (0, 0))
    mat = pl.BlockSpec((d, d), lambda i: (0, 0))
    if pre is not None:
        ins = [h, x, w, row(pre[0]), row(pre[1]), row(bias), row(g), row(b)]
        specs = [tok, tok, mat, vec, vec, vec, vec, vec]
    else:
        ins = [h, x, w, row(bias), row(g), row(b)]
        specs = [tok, tok, mat, vec, vec, vec]
    return pl.pallas_call(
        functools.partial(_proj_ln_body, alpha=alpha, pre_ln=pre is not None),
        grid=(t // tm,),
        in_specs=specs,
        out_specs=[tok, tok, pl.BlockSpec((d, tm), lambda i: (0, i))],
        out_shape=[jax.ShapeDtypeStruct((t, d), F32), jax.ShapeDtypeStruct((t, d), BF16),
                   jax.ShapeDtypeStruct((d, t), BF16)],
        compiler_params=_cparams("parallel"),
        name="proj_ln",
    )(*ins)


def _top_ranks(s, k, exact, want_rank=True):
    n, tm = s.shape
    krow = lax.broadcasted_iota(jnp.int32, (k, tm), 0)
    rank = jnp.full((n, tm), float(k), F32)
    vals = jnp.zeros((k, tm), F32)
    if exact:
        row = lax.broadcasted_iota(jnp.int32, (n, tm), 0).astype(F32)
    for p in range(k):
        m = jnp.max(s, axis=0, keepdims=True)
        hit = s == m
        if exact:
            first = jnp.min(jnp.where(hit, row, float(n)), axis=0, keepdims=True)
            hit = row == first
        if want_rank:
            rank = jnp.where(hit, float(p), rank)
        s = jnp.where(hit, -jnp.inf, s)
        vals = jnp.where(krow == p, m, vals)
    ranked = jnp.sum(jnp.where(rank < float(k), 1.0, 0.0), axis=0, keepdims=True) if want_rank else None
    return vals, rank, ranked


def _pair_counts(a, b, k, exact):
    tm = a.shape[1]
    half = k // 2
    p_full = lax.broadcasted_iota(jnp.int32, (k, tm), 0).astype(F32)
    p_half = lax.broadcasted_iota(jnp.int32, (half, tm), 0).astype(F32)
    cand = [a + b[0:1]]
    flat = [p_full * k]
    for q in range(1, k):
        lim = k // (q + 1)
        cand.append(jnp.where(p_half < lim, a[0:half] + b[q:q + 1], -jnp.inf))
        flat.append(p_half * k + q)
    cand = jnp.concatenate(cand, axis=0)
    flat = jnp.concatenate(flat, axis=0) if exact else None
    picked = jnp.zeros(cand.shape, F32)
    z = jnp.zeros((1, tm), F32)
    best = None
    for r in range(k):
        m = jnp.max(cand, axis=0, keepdims=True)
        best = m if r == 0 else best
        hit = cand == m
        if exact:
            first = jnp.min(jnp.where(hit, flat, float(k * k)), axis=0, keepdims=True)
            hit = flat == first
        picked = jnp.where(hit, 1.0, picked)
        cand = jnp.where(hit, -jnp.inf, cand)
        z = z + jnp.exp(m - best)
    count_lo = picked[0:half]
    for q in range(1, k):
        count_lo = count_lo + picked[k + (q - 1) * half:k + q * half]
    count = jnp.concatenate([count_lo, picked[half:k]], axis=0)
    n_picked = jnp.sum(count, axis=0, keepdims=True)
    return count, z, n_picked


def _route_tile_exact(s0, s1, k):
    a, rank0, _ = _top_ranks(s0, k, True)
    b, rank1, _ = _top_ranks(s1, k, True)
    count, z, _ = _pair_counts(a, b, k, True)
    n0 = jnp.zeros_like(s0)
    for p in range(k):
        n0 = jnp.where(rank0 == float(p), count[p:p + 1], n0)
    e1 = jnp.exp(s1 - b[0:1]) / z
    e0 = jnp.exp(s0 - a[0:1])
    return rank1.astype(BF16), e1.astype(BF16), n0, e0


def _quick_pairs(a, b):
    tm = a.shape[1]
    pf = lax.broadcasted_iota(jnp.int32, (16, tm), 0)
    ph = lax.broadcasted_iota(jnp.int32, (8, tm), 0)
    ninf = -jnp.inf
    return jnp.concatenate([
        b + a[0:1],
        b[0:8] + a[1:2],
        jnp.where(pf >= 2, a + b[0:1], ninf),
        jnp.where(ph >= 2, a[0:8] + b[1:2], ninf),
        jnp.where((ph >= 2) & (ph <= 4), b[0:8] + a[2:3], ninf),
        jnp.where((ph >= 2) & (ph <= 3), b[0:8] + a[3:4], ninf),
        jnp.where(ph == 2, b[0:8] + a[4:5], ninf)], axis=0)


def _route_tile_quick(s0, s1, k):
    assert k == 16
    tm = s0.shape[1]
    kf = float(k)
    a, _, _ = _top_ranks(s0, k, False, want_rank=False)
    b, rank1, ranked1 = _top_ranks(s1, k, False)
    cand = _quick_pairs(a, b)
    z = jnp.zeros((1, tm), F32)
    best = m = None
    for r in range(k):
        m = jnp.max(cand, axis=0, keepdims=True)
        best = m if r == 0 else best
        cand = jnp.where(cand == m, -jnp.inf, cand)
        z = z + jnp.exp(m - best)
    picked = jnp.where(_quick_pairs(a, b) >= m, 1.0, 0.0)
    ph = lax.broadcasted_iota(jnp.int32, (8, tm), 0)
    count_lo = picked[24:32] + picked[40:48]
    for p, rows in enumerate([(0, 16), (16, 24), (48, 56), (56, 64), (64, 72)]):
        count_lo = count_lo + jnp.where(ph == p, jnp.sum(picked[rows[0]:rows[1]], axis=0, keepdims=True), 0.0)
    count = jnp.concatenate([count_lo, picked[32:40]], axis=0)
    n_picked = jnp.sum(count, axis=0, keepdims=True)
    n0 = jnp.zeros_like(s0)
    for p in range(k):
        n0 = jnp.where(s0 == a[p:p + 1], count[p:p + 1], n0)
    ranked0 = jnp.sum(jnp.where(s0 >= a[k - 1:k], 1.0, 0.0), axis=0, keepdims=True)
    e1 = jnp.exp(s1 - b[0:1]) / z
    e0 = jnp.exp(s0 - a[0:1])
    doubt = jnp.where((ranked0 == kf) & (ranked1 == kf) & (n_picked == kf), 0.0, 1.0)
    return rank1.astype(BF16), e1.astype(BF16), n0, e0, doubt


def _pair_scores(q_ref, keys_ref, pair, n_tiles):
    dq = keys_ref.shape[3]
    dn = (((1,), (1,)), ((), ()))
    h = pair // n_tiles
    q = q_ref[h, pl.ds(pl.multiple_of((pair % n_tiles) * LANES, LANES), LANES), :]
    s0 = lax.dot_general(keys_ref[h, 0], q[:, :dq], dn, preferred_element_type=F32)
    s1 = lax.dot_general(keys_ref[h, 1], q[:, dq:], dn, preferred_element_type=F32)
    return s0, s1


def _route_pair(s0, s1, pair, n_tiles, k, outs):
    h = pair // n_tiles
    lanes = pl.ds(pl.multiple_of((pair % n_tiles) * LANES, LANES), LANES)

    def write(vals):
        for ref, val in zip(outs, vals):
            ref[h, :, lanes] = val

    *vals, doubt = _route_tile_quick(s0, s1, k)
    write(vals)

    def redo():
        write(_route_tile_exact(s0, s1, k))

    return redo, jnp.max(doubt)


def _peer_dense_body(q0_ref, qn_ref, keys_ref, xt_ref, u_ref, vt_ref, o_ref,
                     r1_s, e1_s, n0_s, e0_s, new_s, old_s, *, rows, k, group):
    i = pl.program_id(0)
    s = pl.program_id(1)
    n_exp = pl.num_programs(1) - 1
    _, n_heads, n_keys, tb = r1_s.shape
    n_tiles = tb // LANES
    n_pairs = n_heads * n_tiles
    cur = i % 2
    slots = lambda b: [r1_s.at[b], e1_s.at[b], n0_s.at[b], e0_s.at[b]]
    pair_of = lambda g, a: jnp.minimum(g * group + a, n_pairs - 1)

    def route_group(g):
        out = []
        for a in range(group):
            s0, s1 = _pair_scores(qn_ref, keys_ref, pair_of(g, a), n_tiles)
            out.append(_route_pair(s0, s1, pair_of(g, a), n_tiles, k, slots(1 - cur)))
        return out

    @pl.when(s == 0)
    def _():
        o_ref[...] = jnp.zeros_like(o_ref)
        new_s[...] = jnp.zeros_like(new_s)

    @pl.when((i == 0) & (s == 0))
    def _():
        def first(pair, carry):
            s0, s1 = _pair_scores(q0_ref, keys_ref, pair, n_tiles)
            redo, flag = _route_pair(s0, s1, pair, n_tiles, k, slots(0))
            pl.when(flag > 0.0)(redo)
            return carry
        lax.fori_loop(0, n_pairs, first, 0)

    @pl.when(s < n_exp)
    def _():
        old_s[...] = new_s[...]
        pending = route_group(s)
        ht = jnp.dot(u_ref[...], xt_ref[...], preferred_element_type=F32)
        for r in range(rows):
            w = jnp.zeros((n_keys, tb), BF16)
            for h in range(n_heads):
                n_row = jnp.broadcast_to(n0_s[cur, h, pl.ds(s * rows + r, 1), :], (n_keys, tb)).astype(BF16)
                e_row = jnp.broadcast_to(e0_s[cur, h, pl.ds(s * rows + r, 1), :], (n_keys, tb)).astype(BF16)
                w = w + jnp.where(r1_s[cur, h] < n_row, e1_s[cur, h], jnp.zeros((), BF16)) * e_row
            hr = ht[r * n_keys:(r + 1) * n_keys]
            act = 0.5 * hr * (1.0 + lax.erf(hr * (1.0 / math.sqrt(2.0))))
            new_s[r * n_keys:(r + 1) * n_keys, :] = act.astype(BF16) * w
        o_ref[...] += jnp.dot(vt_ref[...], old_s[...], preferred_element_type=F32)
        for redo, flag in pending:
            pl.when(flag > 0.0)(redo)

    @pl.when(s == n_exp)
    def _():
        o_ref[...] += jnp.dot(vt_ref[...], new_s[...], preferred_element_type=F32)


def _peer_dense(q3, keys, xt, u, vt, tb=512, rows=8):
    d, t = xt.shape
    e = u.shape[0]
    n_heads, _, n_keys, dq = keys.shape
    tb = min(tb, t)
    eb = rows * n_keys
    assert tb % LANES == 0
    n_tok, n_exp = t // tb, e // eb
    group = -(-(n_heads * (tb // LANES)) // n_exp)
    slot_bf16 = pltpu.VMEM((2, n_heads, n_keys, tb), BF16)
    slot_f32 = pltpu.VMEM((2, n_heads, n_keys, tb), F32)
    coef = pltpu.VMEM((eb, tb), BF16)
    once = dict(pipeline_mode=pl.Buffered(1))
    return pl.pallas_call(
        functools.partial(_peer_dense_body, rows=rows, k=PEER_TOPK, group=group),
        grid=(n_tok, n_exp + 1),
        in_specs=[pl.BlockSpec((n_heads, tb, 2 * dq), lambda i, s: (0, 0, 0)),
                  pl.BlockSpec((n_heads, tb, 2 * dq), lambda i, s: (0, jnp.minimum(i + 1, n_tok - 1), 0), **once),
                  pl.BlockSpec((n_heads, 2, n_keys, dq), lambda i, s: (0, 0, 0, 0)),
                  pl.BlockSpec((d, tb), lambda i, s: (0, i), **once),
                  pl.BlockSpec((eb, d), lambda i, s: (jnp.minimum(s, n_exp - 1), 0)),
                  pl.BlockSpec((d, eb), lambda i, s: (0, jnp.maximum(s - 1, 0)))],
        out_specs=pl.BlockSpec((d, tb), lambda i, s: (0, i)),
        out_shape=jax.ShapeDtypeStruct((d, t), F32),
        scratch_shapes=[slot_bf16, slot_bf16, slot_f32, slot_f32, coef, coef],
        compiler_params=_cparams("arbitrary", "arbitrary"),
        name="peer_dense",
    )(q3, q3, keys, xt, u, vt)


def _ln_t_body(x_ref, yt_ref, g_ref, b_ref, o_ref, ob_ref, *, alpha):
    x2 = _layer_norm(alpha * x_ref[...] + yt_ref[...].T, g_ref[...], b_ref[...])
    o_ref[...] = x2
    ob_ref[...] = x2.astype(BF16)


def _ln_t(x, yt, g, b, alpha, tm=256):
    t, d = x.shape
    tm = min(tm, t)
    tok = pl.BlockSpec((tm, d), lambda i: (i, 0))
    vec = pl.BlockSpec((1, d), lambda i: (0, 0))
    return pl.pallas_call(
        functools.partial(_ln_t_body, alpha=alpha),
        grid=(t // tm,),
        in_specs=[tok, pl.BlockSpec((d, tm), lambda i: (0, i)), vec, vec],
        out_specs=[tok, tok],
        out_shape=[jax.ShapeDtypeStruct((t, d), F32), jax.ShapeDtypeStruct((t, d), BF16)],
        compiler_params=_cparams("parallel"),
        name="peer_ln",
    )(x, yt, g.reshape(1, d), b.reshape(1, d))


def _peer(x, xb, xt, w_q, sub_keys, u_tab, v_tab, g, b, alpha):
    dq = sub_keys.shape[3]
    q3 = _matmul_heads(xb, w_q.astype(BF16), 2 * dq, name="peer_q")
    yt = _peer_dense(q3, sub_keys.astype(BF16), xt, u_tab.astype(BF16), v_tab.T.astype(BF16))
    return _ln_t(x, yt, g, b, alpha)


def _sb_attn_body(q_ref, k_ref, v_ref, o_ref, acc_ref, run_ref, *, tq, dh, hp, scale):
    qi = pl.program_id(2)
    row = lax.broadcasted_iota(jnp.int32, (tq, tq), 0)
    col = lax.broadcasted_iota(jnp.int32, (tq, tq), 1)
    later = jnp.where(row > col, 1.0, 0.0).astype(BF16)
    causal = col < row
    before = col + jnp.where(qi > 0, 0, 2 * tq) < row + tq

    def block(a, kb, mask, run):
        start = pl.multiple_of(kb * tq, tq)
        lanes = slice(a * dh, (a + 1) * dh)
        kblk = k_ref[pl.ds(start, tq), lanes]
        vblk = v_ref[pl.ds(start, tq), lanes]
        z = lax.dot_general(q_ref[:, lanes], kblk, (((1,), (1,)), ((), ())), preferred_element_type=F32) * scale
        log_b = jnp.minimum(z, 0.0) - jnp.log1p(jnp.exp(-jnp.abs(z)))
        log_1mb = log_b - z
        if mask is not None:
            log_1mb = jnp.where(mask, log_1mb, 0.0)
        hi = log_1mb.astype(BF16)
        lo = (log_1mb - hi.astype(F32)).astype(BF16)
        rest = (jnp.dot(hi, later, preferred_element_type=F32)
                + jnp.dot(lo, later, preferred_element_type=F32))
        att = jnp.exp(log_b + rest + run)
        if mask is not None:
            att = jnp.where(mask, att, 0.0)
        pv = jnp.dot(att.astype(BF16), vblk, preferred_element_type=F32)
        return pv, run + rest[:, 0:1] + log_1mb[:, 0:1]

    top = None
    for a in range(hp):
        pv0, run = block(a, qi, causal, jnp.zeros((tq, 1), F32))
        pv1, run = block(a, jnp.maximum(qi - 1, 0), before, run)
        acc_ref[a] = pv0 + pv1
        run_ref[a] = run
        top = jnp.max(run) if top is None else jnp.maximum(top, jnp.max(run))

    def cond(c):
        kb, top = c
        return (kb >= 0) & (top > SB_LOG_CUTOFF)

    def body(c):
        kb, _ = c
        top = None
        for a in range(hp):
            pv, run = block(a, kb, None, run_ref[a])
            acc_ref[a] += pv
            run_ref[a] = run
            top = jnp.max(run) if top is None else jnp.maximum(top, jnp.max(run))
        return kb - 1, top

    lax.while_loop(cond, body, (qi - 2, top))
    for a in range(hp):
        o_ref[:, a * dh:(a + 1) * dh] = acc_ref[a].astype(o_ref.dtype)


def _sb_attention(q3, kv3, n_heads, tq=256, hp=4):
    b, s, d = q3.shape
    dh = d // n_heads
    tq = min(tq, s)
    hp = min(hp, n_heads)
    groups = n_heads // hp
    return pl.pallas_call(
        functools.partial(_sb_attn_body, tq=tq, dh=dh, hp=hp, scale=dh ** -0.5),
        grid=(b, groups, s // tq),
        in_specs=[pl.BlockSpec((None, tq, hp * dh), lambda bi, h, i: (bi, i, h)),
                  pl.BlockSpec((None, s, hp * dh), lambda bi, h, i: (bi, 0, h)),
                  pl.BlockSpec((None, s, hp * dh), lambda bi, h, i: (bi, 0, h + groups))],
        out_specs=pl.BlockSpec((None, tq, hp * dh), lambda bi, h, i: (bi, i, h)),
        out_shape=jax.ShapeDtypeStruct((b, s, d), BF16),
        scratch_shapes=[pltpu.VMEM((hp, tq, dh), F32), pltpu.VMEM((hp, tq, 1), F32)],
        compiler_params=_cparams("parallel", "parallel", "arbitrary"),
        name="sb_attn",
    )(q3, kv3, kv3)


def kernel(x, conv_w_in, conv_b_in, conv_w_dw, conv_b_dw, conv_ln_g, conv_ln_b, conv_w_out, conv_b_out,
           attn_w_kv, attn_w_q, attn_w_o, peer_w_q, peer_sub_keys, peer_u, peer_v, ln_g, ln_b):
    bsz, seq, d = x.shape
    t = bsz * seq
    depth = peer_w_q.shape[0]
    n_a = conv_w_in.shape[0]
    alpha = (2.0 * depth) ** 0.25
    n_sb_heads = d // SB_HEAD_DIM

    xf = x.reshape(t, d)
    kv3 = None
    for i in range(depth):
        if i < n_a:
            g = _glu(xf, conv_w_in[i].astype(BF16), conv_b_in[i])
            c = _dwconv(g.reshape(bsz, seq, d), conv_w_dw[i].reshape(-1, d), conv_b_dw[i]).reshape(t, d)
            x1, x1b, x1t = _proj_ln(c, xf, conv_w_out[i].astype(BF16), conv_b_out[i], ln_g[i, 0], ln_b[i, 0],
                                    alpha, pre=(conv_ln_g[i], conv_ln_b[i]))
        else:
            j = i - n_a
            q = _matmul(xb, attn_w_q[j].astype(BF16), BF16, name="attn_q")
            o = _sb_attention(q.reshape(bsz, seq, d), kv3, n_sb_heads).reshape(t, d)
            x1, x1b, x1t = _proj_ln(o, xf, attn_w_o[j].astype(BF16), jnp.zeros((d,), F32), ln_g[i, 0],
                                    ln_b[i, 0], alpha)
        xf, xb = _peer(x1, x1b, x1t, peer_w_q[i], peer_sub_keys[i], peer_u[i], peer_v[i],
                       ln_g[i, 1], ln_b[i, 1], alpha)
        if i == n_a - 1:
            kv3 = _matmul(xb, attn_w_kv.astype(BF16), BF16, name="attn_kv").reshape(bsz, seq, 2 * d)
    return xf.reshape(bsz, seq, d)
```

```python
import functools
import math

import jax
import jax.numpy as jnp
from jax import lax
from jax.experimental import pallas as pl
from jax.experimental.pallas import tpu as pltpu

F32 = jnp.float32
BF16 = jnp.bfloat16

LN_EPS = 1e-5
LANES = 128
SUBLANES = 8
CONV_HALO = 32
PEER_TOPK = 16
SB_HEAD_DIM = 128
SB_LOG_CUTOFF = -104.0
VMEM_LIMIT_BYTES = 56 * 1024 * 1024


def _cparams(*sem):
    return pltpu.CompilerParams(dimension_semantics=sem, vmem_limit_bytes=VMEM_LIMIT_BYTES)


def _layer_norm(v, g, b):
    mu = jnp.mean(v, axis=-1, keepdims=True)
    c = v - mu
    var = jnp.mean(c * c, axis=-1, keepdims=True)
    return c * lax.rsqrt(var + LN_EPS) * g + b


def _mm_body(x_ref, w_ref, o_ref):
    o_ref[...] = jnp.dot(x_ref[...], w_ref[...], preferred_element_type=F32).astype(o_ref.dtype)


def _matmul(x, w, out_dtype, tm=1024, tn=1024, name="matmul"):
    m, k = x.shape
    n = w.shape[1]
    tm, tn = min(tm, m), min(tn, n)
    return pl.pallas_call(
        _mm_body,
        grid=(m // tm, n // tn),
        in_specs=[pl.BlockSpec((tm, k), lambda i, j: (i, 0)),
                  pl.BlockSpec((k, tn), lambda i, j: (0, j))],
        out_specs=pl.BlockSpec((tm, tn), lambda i, j: (i, j)),
        out_shape=jax.ShapeDtypeStruct((m, n), out_dtype),
        compiler_params=_cparams("parallel", "parallel"),
        name=name,
    )(x, w)


def _matmul_heads(x, w, width, tm=1024, name="matmul_heads"):
    m, k = x.shape
    n_heads = w.shape[1] // width
    tm = min(tm, m)
    return pl.pallas_call(
        _mm_body,
        grid=(m // tm, n_heads),
        in_specs=[pl.BlockSpec((tm, k), lambda i, j: (i, 0)),
                  pl.BlockSpec((k, width), lambda i, j: (0, j))],
        out_specs=pl.BlockSpec((None, tm, width), lambda i, j: (j, i, 0)),
        out_shape=jax.ShapeDtypeStruct((n_heads, m, width), BF16),
        compiler_params=_cparams("parallel", "parallel"),
        name=name,
    )(x, w)


def _glu_body(x_ref, wa_ref, wg_ref, ba_ref, bg_ref, o_ref):
    xb = x_ref[...].astype(BF16)
    a = jnp.dot(xb, wa_ref[...], preferred_element_type=F32) + ba_ref[...]
    gate = jnp.dot(xb, wg_ref[...], preferred_element_type=F32) + bg_ref[...]
    o_ref[...] = a * jax.nn.sigmoid(gate)


def _glu(x, w_in, b_in, tm=1024, tn=512):
    t, d = x.shape
    tm, tn = min(tm, t), min(tn, d)
    nb = d // tn
    b2 = b_in.reshape(1, 2 * d)
    return pl.pallas_call(
        _glu_body,
        grid=(t // tm, nb),
        in_specs=[pl.BlockSpec((tm, d), lambda i, j: (i, 0)),
                  pl.BlockSpec((d, tn), lambda i, j: (0, j)),
                  pl.BlockSpec((d, tn), lambda i, j: (0, j + nb)),
                  pl.BlockSpec((1, tn), lambda i, j: (0, j)),
                  pl.BlockSpec((1, tn), lambda i, j: (0, j + nb))],
        out_specs=pl.BlockSpec((tm, tn), lambda i, j: (i, j)),
        out_shape=jax.ShapeDtypeStruct((t, d), F32),
        compiler_params=_cparams("parallel", "parallel"),
        name="conv_glu",
    )(x, w_in, w_in, b2, b2)


def _dwconv_body(halo_ref, cur_ref, w_ref, b_ref, o_ref, buf_ref, shift_ref, *, width, ts):
    i = pl.program_id(1)
    halo = halo_ref[...]
    buf_ref[0:CONV_HALO, :] = jnp.where(i > 0, halo, jnp.zeros_like(halo))
    buf_ref[CONV_HALO:CONV_HALO + ts, :] = cur_ref[...]
    acc = jnp.broadcast_to(b_ref[...], o_ref.shape)
    first = CONV_HALO - (width - 1)
    for r in range(min(SUBLANES, width)):
        taps = range(r, width, SUBLANES)
        span = ts + (len(taps) - 1) * SUBLANES
        shift_ref[r, 0:span, :] = buf_ref[first + r:first + r + span, :]
        for a, k in enumerate(taps):
            acc = acc + w_ref[k:k + 1, :] * shift_ref[r, a * SUBLANES:a * SUBLANES + ts, :]
    o_ref[...] = acc


def _dwconv(g3, w_dw, b_dw, ts=512, td=256):
    b, s, d = g3.shape
    width = w_dw.shape[0]
    assert width <= CONV_HALO
    ts, td = min(ts, s), min(td, d)
    per = ts // CONV_HALO
    return pl.pallas_call(
        functools.partial(_dwconv_body, width=width, ts=ts),
        grid=(b, s // ts, d // td),
        in_specs=[pl.BlockSpec((None, CONV_HALO, td), lambda bi, i, j: (bi, jnp.maximum(i * per - 1, 0), j)),
                  pl.BlockSpec((None, ts, td), lambda bi, i, j: (bi, i, j)),
                  pl.BlockSpec((width, td), lambda bi, i, j: (0, j)),
                  pl.BlockSpec((1, td), lambda bi, i, j: (0, j))],
        out_specs=pl.BlockSpec((None, ts, td), lambda bi, i, j: (bi, i, j)),
        out_shape=jax.ShapeDtypeStruct((b, s, d), F32),
        scratch_shapes=[pltpu.VMEM((CONV_HALO + ts, td), F32),
                        pltpu.VMEM((SUBLANES, ts + CONV_HALO - SUBLANES, td), F32)],
        compiler_params=_cparams("parallel", "parallel", "parallel"),
        name="conv_dw",
    )(g3, g3, w_dw, b_dw.reshape(1, d))


def _proj_ln_body(*refs, alpha, pre_ln):
    if pre_ln:
        h_ref, x_ref, w_ref, pg_ref, pb_ref, bias_ref, g_ref, b_ref, o_ref, ob_ref, ot_ref = refs
        h = _layer_norm(h_ref[...], pg_ref[...], pb_ref[...])
        h = (h * jax.nn.sigmoid(h)).astype(BF16)
    else:
        h_ref, x_ref, w_ref, bias_ref, g_ref, b_ref, o_ref, ob_ref, ot_ref = refs
        h = h_ref[...]
    y = jnp.dot(h, w_ref[...], preferred_element_type=F32) + bias_ref[...]
    x1 = _layer_norm(alpha * x_ref[...] + y, g_ref[...], b_ref[...])
    o_ref[...] = x1
    ob_ref[...] = x1.astype(BF16)
    ot_ref[...] = x1.T.astype(BF16)


def _proj_ln(h, x, w, bias, g, b, alpha, pre=None, tm=256):
    t, d = x.shape
    tm = min(tm, t)
    row = lambda v: v.reshape(1, d)
    tok = pl.BlockSpec((tm, d), lambda i: (i, 0))
    vec = pl.BlockSpec((1, d), lambda i: (0, 0))
    mat = pl.BlockSpec((d, d), lambda i: (0, 0))
    if pre is not None:
        ins = [h, x, w, row(pre[0]), row(pre[1]), row(bias), row(g), row(b)]
        specs = [tok, tok, mat, vec, vec, vec, vec, vec]
    else:
        ins = [h, x, w, row(bias), row(g), row(b)]
        specs = [tok, tok, mat, vec, vec, vec]
    return pl.pallas_call(
        functools.partial(_proj_ln_body, alpha=alpha, pre_ln=pre is not None),
        grid=(t // tm,),
        in_specs=specs,
        out_specs=[tok, tok, pl.BlockSpec((d, tm), lambda i: (0, i))],
        out_shape=[jax.ShapeDtypeStruct((t, d), F32), jax.ShapeDtypeStruct((t, d), BF16),
                   jax.ShapeDtypeStruct((d, t), BF16)],
        compiler_params=_cparams("parallel"),
        name="proj_ln",
    )(*ins)


def _top_ranks(s, k, exact, want_rank=True):
    n, tm = s.shape
    krow = lax.broadcasted_iota(jnp.int32, (k, tm), 0)
    rank = jnp.full((n, tm), float(k), F32)
    vals = jnp.zeros((k, tm), F32)
    if exact:
        row = lax.broadcasted_iota(jnp.int32, (n, tm), 0).astype(F32)
    for p in range(k):
        m = jnp.max(s, axis=0, keepdims=True)
        hit = s == m
        if exact:
            first = jnp.min(jnp.where(hit, row, float(n)), axis=0, keepdims=True)
            hit = row == first
        if want_rank:
            rank = jnp.where(hit, float(p), rank)
        s = jnp.where(hit, -jnp.inf, s)
        vals = jnp.where(krow == p, m, vals)
    ranked = jnp.sum(jnp.where(rank < float(k), 1.0, 0.0), axis=0, keepdims=True) if want_rank else None
    return vals, rank, ranked


def _pair_counts(a, b, k, exact):
    tm = a.shape[1]
    half = k // 2
    p_full = lax.broadcasted_iota(jnp.int32, (k, tm), 0).astype(F32)
    p_half = lax.broadcasted_iota(jnp.int32, (half, tm), 0).astype(F32)
    cand = [a + b[0:1]]
    flat = [p_full * k]
    for q in range(1, k):
        lim = k // (q + 1)
        cand.append(jnp.where(p_half < lim, a[0:half] + b[q:q + 1], -jnp.inf))
        flat.append(p_half * k + q)
    cand = jnp.concatenate(cand, axis=0)
    flat = jnp.concatenate(flat, axis=0) if exact else None
    picked = jnp.zeros(cand.shape, F32)
    z = jnp.zeros((1, tm), F32)
    best = None
    for r in range(k):
        m = jnp.max(cand, axis=0, keepdims=True)
        best = m if r == 0 else best
        hit = cand == m
        if exact:
            first = jnp.min(jnp.where(hit, flat, float(k * k)), axis=0, keepdims=True)
            hit = flat == first
        picked = jnp.where(hit, 1.0, picked)
        cand = jnp.where(hit, -jnp.inf, cand)
        z = z + jnp.exp(m - best)
    count_lo = picked[0:half]
    for q in range(1, k):
        count_lo = count_lo + picked[k + (q - 1) * half:k + q * half]
    count = jnp.concatenate([count_lo, picked[half:k]], axis=0)
    n_picked = jnp.sum(count, axis=0, keepdims=True)
    return count, z, n_picked


def _route_tile_exact(s0, s1, k):
    a, rank0, _ = _top_ranks(s0, k, True)
    b, rank1, _ = _top_ranks(s1, k, True)
    count, z, _ = _pair_counts(a, b, k, True)
    n0 = jnp.zeros_like(s0)
    for p in range(k):
        n0 = jnp.where(rank0 == float(p), count[p:p + 1], n0)
    e1 = jnp.exp(s1 - b[0:1]) / z
    e0 = jnp.exp(s0 - a[0:1])
    return rank1.astype(BF16), e1.astype(BF16), n0, e0


def _quick_pairs(a, b):
    tm = a.shape[1]
    pf = lax.broadcasted_iota(jnp.int32, (16, tm), 0)
    ph = lax.broadcasted_iota(jnp.int32, (8, tm), 0)
    ninf = -jnp.inf
    return jnp.concatenate([
        b + a[0:1],
        b[0:8] + a[1:2],
        jnp.where(pf >= 2, a + b[0:1], ninf),
        jnp.where(ph >= 2, a[0:8] + b[1:2], ninf),
        jnp.where((ph >= 2) & (ph <= 4), b[0:8] + a[2:3], ninf),
        jnp.where((ph >= 2) & (ph <= 3), b[0:8] + a[3:4], ninf),
        jnp.where(ph == 2, b[0:8] + a[4:5], ninf)], axis=0)


def _route_tile_quick(s0, s1, k):
    assert k == 16
    tm = s0.shape[1]
    kf = float(k)
    a, _, _ = _top_ranks(s0, k, False, want_rank=False)
    b, rank1, ranked1 = _top_ranks(s1, k, False)
    cand = _quick_pairs(a, b)
    z = jnp.zeros((1, tm), F32)
    best = m = None
    for r in range(k):
        m = jnp.max(cand, axis=0, keepdims=True)
        best = m if r == 0 else best
        cand = jnp.where(cand == m, -jnp.inf, cand)
        z = z + jnp.exp(m - best)
    picked = jnp.where(_quick_pairs(a, b) >= m, 1.0, 0.0)
    ph = lax.broadcasted_iota(jnp.int32, (8, tm), 0)
    count_lo = picked[24:32] + picked[40:48]
    for p, rows in enumerate([(0, 16), (16, 24), (48, 56), (56, 64), (64, 72)]):
        count_lo = count_lo + jnp.where(ph == p, jnp.sum(picked[rows[0]:rows[1]], axis=0, keepdims=True), 0.0)
    count = jnp.concatenate([count_lo, picked[32:40]], axis=0)
    n_picked = jnp.sum(count, axis=0, keepdims=True)
    n0 = jnp.zeros_like(s0)
    for p in range(k):
        n0 = jnp.where(s0 == a[p:p + 1], count[p:p + 1], n0)
    ranked0 = jnp.sum(jnp.where(s0 >= a[k - 1:k], 1.0, 0.0), axis=0, keepdims=True)
    e1 = jnp.exp(s1 - b[0:1]) / z
    e0 = jnp.exp(s0 - a[0:1])
    doubt = jnp.where((ranked0 == kf) & (ranked1 == kf) & (n_picked == kf), 0.0, 1.0)
    return rank1.astype(BF16), e1.astype(BF16), n0, e0, doubt


def _pair_scores(q_ref, keys_ref, pair, n_tiles):
    dq = keys_ref.shape[3]
    dn = (((1,), (1,)), ((), ()))
    h = pair // n_tiles
    q = q_ref[h, pl.ds(pl.multiple_of((pair % n_tiles) * LANES, LANES), LANES), :]
    s0 = lax.dot_general(keys_ref[h, 0], q[:, :dq], dn, preferred_element_type=F32)
    s1 = lax.dot_general(keys_ref[h, 1], q[:, dq:], dn, preferred_element_type=F32)
    return s0, s1


def _route_pair(s0, s1, pair, n_tiles, k, outs):
    h = pair // n_tiles
    lanes = pl.ds(pl.multiple_of((pair % n_tiles) * LANES, LANES), LANES)

    def write(vals):
        for ref, val in zip(outs, vals):
            ref[h, :, lanes] = val

    *vals, doubt = _route_tile_quick(s0, s1, k)
    write(vals)

    def redo():
        write(_route_tile_exact(s0, s1, k))

    return redo, jnp.max(doubt)


def _peer_dense_body(q0_ref, qn_ref, keys_ref, xt_ref, u_ref, vt_ref, o_ref,
                     r1_s, e1_s, n0_s, e0_s, new_s, old_s, *, rows, k, group):
    i = pl.program_id(0)
    s = pl.program_id(1)
    n_exp = pl.num_programs(1) - 1
    _, n_heads, n_keys, tb = r1_s.shape
    n_tiles = tb // LANES
    n_pairs = n_heads * n_tiles
    cur = i % 2
    slots = lambda b: [r1_s.at[b], e1_s.at[b], n0_s.at[b], e0_s.at[b]]
    pair_of = lambda g, a: jnp.minimum(g * group + a, n_pairs - 1)

    def route_group(g):
        out = []
        for a in range(group):
            s0, s1 = _pair_scores(qn_ref, keys_ref, pair_of(g, a), n_tiles)
            out.append(_route_pair(s0, s1, pair_of(g, a), n_tiles, k, slots(1 - cur)))
        return out

    @pl.when(s == 0)
    def _():
        o_ref[...] = jnp.zeros_like(o_ref)
        new_s[...] = jnp.zeros_like(new_s)

    @pl.when((i == 0) & (s == 0))
    def _():
        def first(pair, carry):
            s0, s1 = _pair_scores(q0_ref, keys_ref, pair, n_tiles)
            redo, flag = _route_pair(s0, s1, pair, n_tiles, k, slots(0))
            pl.when(flag > 0.0)(redo)
            return carry
        lax.fori_loop(0, n_pairs, first, 0)

    @pl.when(s < n_exp)
    def _():
        old_s[...] = new_s[...]
        pending = route_group(s)
        ht = jnp.dot(u_ref[...], xt_ref[...], preferred_element_type=F32)
        for r in range(rows):
            w = jnp.zeros((n_keys, tb), BF16)
            for h in range(n_heads):
                n_row = jnp.broadcast_to(n0_s[cur, h, pl.ds(s * rows + r, 1), :], (n_keys, tb)).astype(BF16)
                e_row = jnp.broadcast_to(e0_s[cur, h, pl.ds(s * rows + r, 1), :], (n_keys, tb)).astype(BF16)
                w = w + jnp.where(r1_s[cur, h] < n_row, e1_s[cur, h], jnp.zeros((), BF16)) * e_row
            hr = ht[r * n_keys:(r + 1) * n_keys]
            act = 0.5 * hr * (1.0 + lax.erf(hr * (1.0 / math.sqrt(2.0))))
            new_s[r * n_keys:(r + 1) * n_keys, :] = act.astype(BF16) * w
        o_ref[...] += jnp.dot(vt_ref[...], old_s[...], preferred_element_type=F32)
        for redo, flag in pending:
            pl.when(flag > 0.0)(redo)

    @pl.when(s == n_exp)
    def _():
        o_ref[...] += jnp.dot(vt_ref[...], new_s[...], preferred_element_type=F32)


def _peer_dense(q3, keys, xt, u, v, tb=512, rows=8):
    d, t = xt.shape
    e = u.shape[0]
    n_heads, _, n_keys, dq = keys.shape
    tb = min(tb, t)
    eb = rows * n_keys
    assert tb % LANES == 0
    n_tok, n_exp = t // tb, e // eb
    group = -(-(n_heads * (tb // LANES)) // n_exp)
    vt = v.reshape(n_exp, eb, d).transpose(0, 2, 1)
    slot_bf16 = pltpu.VMEM((2, n_heads, n_keys, tb), BF16)
    slot_f32 = pltpu.VMEM((2, n_heads, n_keys, tb), F32)
    coef = pltpu.VMEM((eb, tb), BF16)
    once = dict(pipeline_mode=pl.Buffered(1))
    return pl.pallas_call(
        functools.partial(_peer_dense_body, rows=rows, k=PEER_TOPK, group=group),
        grid=(n_tok, n_exp + 1),
        in_specs=[pl.BlockSpec((n_heads, tb, 2 * dq), lambda i, s: (0, 0, 0)),
                  pl.BlockSpec((n_heads, tb, 2 * dq), lambda i, s: (0, jnp.minimum(i + 1, n_tok - 1), 0), **once),
                  pl.BlockSpec((n_heads, 2, n_keys, dq), lambda i, s: (0, 0, 0, 0)),
                  pl.BlockSpec((d, tb), lambda i, s: (0, i), **once),
                  pl.BlockSpec((eb, d), lambda i, s: (jnp.minimum(s, n_exp - 1), 0)),
                  pl.BlockSpec((None, d, eb), lambda i, s: (jnp.maximum(s - 1, 0), 0, 0))],
        out_specs=pl.BlockSpec((d, tb), lambda i, s: (0, i)),
        out_shape=jax.ShapeDtypeStruct((d, t), F32),
        scratch_shapes=[slot_bf16, slot_bf16, slot_f32, slot_f32, coef, coef],
        compiler_params=_cparams("arbitrary", "arbitrary"),
        name="peer_dense",
    )(q3, q3, keys, xt, u, vt)


def _ln_t_body(x_ref, yt_ref, g_ref, b_ref, o_ref, ob_ref, *, alpha):
    x2 = _layer_norm(alpha * x_ref[...] + yt_ref[...].T, g_ref[...], b_ref[...])
    o_ref[...] = x2
    ob_ref[...] = x2.astype(BF16)


def _ln_t(x, yt, g, b, alpha, tm=256):
    t, d = x.shape
    tm = min(tm, t)
    tok = pl.BlockSpec((tm, d), lambda i: (i, 0))
    vec = pl.BlockSpec((1, d), lambda i: (0, 0))
    return pl.pallas_call(
        functools.partial(_ln_t_body, alpha=alpha),
        grid=(t // tm,),
        in_specs=[tok, pl.BlockSpec((d, tm), lambda i: (0, i)), vec, vec],
        out_specs=[tok, tok],
        out_shape=[jax.ShapeDtypeStruct((t, d), F32), jax.ShapeDtypeStruct((t, d), BF16)],
        compiler_params=_cparams("parallel"),
        name="peer_ln",
    )(x, yt, g.reshape(1, d), b.reshape(1, d))


def _peer(x, xb, xt, w_q, sub_keys, u_tab, v_tab, g, b, alpha):
    dq = sub_keys.shape[3]
    q3 = _matmul_heads(xb, w_q.astype(BF16), 2 * dq, name="peer_q")
    yt = _peer_dense(q3, sub_keys.astype(BF16), xt, u_tab.astype(BF16), v_tab.astype(BF16))
    return _ln_t(x, yt, g, b, alpha)


def _sb_attn_body(q_ref, k_ref, v_ref, o_ref, acc_ref, run_ref, *, tq, dh, hp, scale):
    qi = pl.program_id(2)
    row = lax.broadcasted_iota(jnp.int32, (tq, tq), 0)
    col = lax.broadcasted_iota(jnp.int32, (tq, tq), 1)
    later = jnp.where(row > col, 1.0, 0.0).astype(BF16)
    causal = col < row

    def block(a, kb, mask, run):
        start = pl.multiple_of(kb * tq, tq)
        lanes = slice(a * dh, (a + 1) * dh)
        kblk = k_ref[pl.ds(start, tq), lanes]
        vblk = v_ref[pl.ds(start, tq), lanes]
        z = lax.dot_general(q_ref[:, lanes], kblk, (((1,), (1,)), ((), ())), preferred_element_type=F32) * scale
        log_b = jnp.minimum(z, 0.0) - jnp.log1p(jnp.exp(-jnp.abs(z)))
        log_1mb = log_b - z
        if mask is not None:
            log_1mb = jnp.where(mask, log_1mb, 0.0)
        hi = log_1mb.astype(BF16)
        lo = (log_1mb - hi.astype(F32)).astype(BF16)
        rest = (jnp.dot(hi, later, preferred_element_type=F32)
                + jnp.dot(lo, later, preferred_element_type=F32))
        att = jnp.exp(log_b + rest + run)
        if mask is not None:
            att = jnp.where(mask, att, 0.0)
        pv = jnp.dot(att.astype(BF16), vblk, preferred_element_type=F32)
        return pv, run + rest[:, 0:1] + log_1mb[:, 0:1]

    top = None
    for a in range(hp):
        pv0, run = block(a, qi, causal, jnp.zeros((tq, 1), F32))
        run = jnp.where(qi > 0, run, -jnp.inf)
        pv1, run = block(a, jnp.maximum(qi - 1, 0), None, run)
        acc_ref[a] = pv0 + pv1
        run_ref[a] = run
        top = jnp.max(run) if top is None else jnp.maximum(top, jnp.max(run))

    def cond(c):
        kb, top = c
        return (kb >= 0) & (top > SB_LOG_CUTOFF)

    def body(c):
        kb, _ = c
        top = None
        for a in range(hp):
            pv, run = block(a, kb, None, run_ref[a])
            acc_ref[a] += pv
            run_ref[a] = run
            top = jnp.max(run) if top is None else jnp.maximum(top, jnp.max(run))
        return kb - 1, top

    lax.while_loop(cond, body, (qi - 2, top))
    for a in range(hp):
        o_ref[:, a * dh:(a + 1) * dh] = acc_ref[a].astype(o_ref.dtype)


def _sb_attention(q3, kv3, n_heads, tq=256, hp=4):
    b, s, d = q3.shape
    dh = d // n_heads
    tq = min(tq, s)
    hp = min(hp, n_heads)
    groups = n_heads // hp
    return pl.pallas_call(
        functools.partial(_sb_attn_body, tq=tq, dh=dh, hp=hp, scale=dh ** -0.5),
        grid=(b, groups, s // tq),
        in_specs=[pl.BlockSpec((None, tq, hp * dh), lambda bi, h, i: (bi, i, h)),
                  pl.BlockSpec((None, s, hp * dh), lambda bi, h, i: (bi, 0, h)),
                  pl.BlockSpec((None, s, hp * dh), lambda bi, h, i: (bi, 0, h + groups))],
        out_specs=pl.BlockSpec((None, tq, hp * dh), lambda bi, h, i: (bi, i, h)),
        out_shape=jax.ShapeDtypeStruct((b, s, d), BF16),
        scratch_shapes=[pltpu.VMEM((hp, tq, dh), F32), pltpu.VMEM((hp, tq, 1), F32)],
        compiler_params=_cparams("parallel", "parallel", "arbitrary"),
        name="sb_attn",
    )(q3, kv3, kv3)


def kernel(x, conv_w_in, conv_b_in, conv_w_dw, conv_b_dw, conv_ln_g, conv_ln_b, conv_w_out, conv_b_out,
           attn_w_kv, attn_w_q, attn_w_o, peer_w_q, peer_sub_keys, peer_u, peer_v, ln_g, ln_b):
    bsz, seq, d = x.shape
    t = bsz * seq
    depth = peer_w_q.shape[0]
    n_a = conv_w_in.shape[0]
    alpha = (2.0 * depth) ** 0.25
    n_sb_heads = d // SB_HEAD_DIM

    xf = x.reshape(t, d)
    kv3 = None
    for i in range(depth):
        if i < n_a:
            g = _glu(xf, conv_w_in[i].astype(BF16), conv_b_in[i])
            c = _dwconv(g.reshape(bsz, seq, d), conv_w_dw[i].reshape(-1, d), conv_b_dw[i]).reshape(t, d)
            x1, x1b, x1t = _proj_ln(c, xf, conv_w_out[i].astype(BF16), conv_b_out[i], ln_g[i, 0], ln_b[i, 0],
                                    alpha, pre=(conv_ln_g[i], conv_ln_b[i]))
        else:
            j = i - n_a
            q = _matmul(xb, attn_w_q[j].astype(BF16), BF16, name="attn_q")
            o = _sb_attention(q.reshape(bsz, seq, d), kv3, n_sb_heads).reshape(t, d)
            x1, x1b, x1t = _proj_ln(o, xf, attn_w_o[j].astype(BF16), jnp.zeros((d,), F32), ln_g[i, 0],
                                    ln_b[i, 0], alpha)
        xf, xb = _peer(x1, x1b, x1t, peer_w_q[i], peer_sub_keys[i], peer_u[i], peer_v[i],
                       ln_g[i, 1], ln_b[i, 1], alpha)
        if i == n_a - 1:
            kv3 = _matmul(xb, attn_w_kv.astype(BF16), BF16, name="attn_kv").reshape(bsz, seq, 2 * d)
    return xf.reshape(bsz, seq, d)
```

```python
import functools
import math

import jax
import jax.numpy as jnp
from jax import lax
from jax.experimental import pallas as pl
from jax.experimental.pallas import tpu as pltpu

F32 = jnp.float32
BF16 = jnp.bfloat16

LN_EPS = 1e-5
LANES = 128
SUBLANES = 8
CONV_HALO = 32
PEER_TOPK = 16
PEER_BLOCK_ROWS = 8
SB_HEAD_DIM = 128
SB_LOG_CUTOFF = -104.0
VMEM_LIMIT_BYTES = 56 * 1024 * 1024


def _cparams(*sem):
    return pltpu.CompilerParams(dimension_semantics=sem, vmem_limit_bytes=VMEM_LIMIT_BYTES)


def _layer_norm(v, g, b):
    mu = jnp.mean(v, axis=-1, keepdims=True)
    c = v - mu
    var = jnp.mean(c * c, axis=-1, keepdims=True)
    return c * lax.rsqrt(var + LN_EPS) * g + b


def _mm_body(x_ref, w_ref, o_ref):
    o_ref[...] = jnp.dot(x_ref[...], w_ref[...], preferred_element_type=F32).astype(o_ref.dtype)


def _matmul(x, w, out_dtype, tm=1024, tn=1024, name="matmul"):
    m, k = x.shape
    n = w.shape[1]
    tm, tn = min(tm, m), min(tn, n)
    return pl.pallas_call(
        _mm_body,
        grid=(m // tm, n // tn),
        in_specs=[pl.BlockSpec((tm, k), lambda i, j: (i, 0)),
                  pl.BlockSpec((k, tn), lambda i, j: (0, j))],
        out_specs=pl.BlockSpec((tm, tn), lambda i, j: (i, j)),
        out_shape=jax.ShapeDtypeStruct((m, n), out_dtype),
        compiler_params=_cparams("parallel", "parallel"),
        name=name,
    )(x, w)


def _mm_heads_body(x_ref, w_ref, o_ref):
    hb, _, width = o_ref.shape
    res = jnp.dot(x_ref[...], w_ref[...], preferred_element_type=F32).astype(o_ref.dtype)
    for h in range(hb):
        o_ref[h] = res[:, h * width:(h + 1) * width]


def _matmul_heads(x, w, width, tm=1024, hb=4, name="matmul_heads"):
    m, k = x.shape
    n_heads = w.shape[1] // width
    tm = min(tm, m)
    hb = math.gcd(hb, n_heads)
    return pl.pallas_call(
        _mm_heads_body,
        grid=(m // tm, n_heads // hb),
        in_specs=[pl.BlockSpec((tm, k), lambda i, j: (i, 0)),
                  pl.BlockSpec((k, hb * width), lambda i, j: (0, j))],
        out_specs=pl.BlockSpec((hb, tm, width), lambda i, j: (j, i, 0)),
        out_shape=jax.ShapeDtypeStruct((n_heads, m, width), BF16),
        compiler_params=_cparams("parallel", "parallel"),
        name=name,
    )(x, w)


def _cast_body(x_ref, o_ref):
    o_ref[...] = x_ref[...].astype(o_ref.dtype)


def _cast_t_body(x_ref, o_ref):
    o_ref[...] = x_ref[...].T.astype(o_ref.dtype)


def _table_bf16(tab, layer, rows=1024):
    _, e, d = tab.shape
    rows = min(rows, e)
    return pl.pallas_call(
        _cast_body,
        grid=(e // rows,),
        in_specs=[pl.BlockSpec((None, rows, d), lambda j: (layer, j, 0))],
        out_specs=pl.BlockSpec((rows, d), lambda j: (j, 0)),
        out_shape=jax.ShapeDtypeStruct((e, d), BF16),
        compiler_params=_cparams("parallel"),
        name="table_bf16",
    )(tab)


def _table_blocks_t_bf16(tab, layer, eb, rows=512):
    _, e, d = tab.shape
    rows = min(rows, eb)
    per = eb // rows
    return pl.pallas_call(
        _cast_t_body,
        grid=(e // eb, per),
        in_specs=[pl.BlockSpec((None, rows, d), lambda j, c: (layer, j * per + c, 0))],
        out_specs=pl.BlockSpec((None, d, rows), lambda j, c: (j, 0, c)),
        out_shape=jax.ShapeDtypeStruct((e // eb, d, eb), BF16),
        compiler_params=_cparams("parallel", "parallel"),
        name="table_t_bf16",
    )(tab)


def _glu_body(x_ref, wa_ref, wg_ref, ba_ref, bg_ref, o_ref):
    xb = x_ref[...].astype(BF16)
    a = jnp.dot(xb, wa_ref[...], preferred_element_type=F32) + ba_ref[...]
    gate = jnp.dot(xb, wg_ref[...], preferred_element_type=F32) + bg_ref[...]
    o_ref[...] = a * jax.nn.sigmoid(gate)


def _glu(x, w_in, b_in, tm=1024, tn=512):
    t, d = x.shape
    tm, tn = min(tm, t), min(tn, d)
    nb = d // tn
    b2 = b_in.reshape(1, 2 * d)
    return pl.pallas_call(
        _glu_body,
        grid=(t // tm, nb),
        in_specs=[pl.BlockSpec((tm, d), lambda i, j: (i, 0)),
                  pl.BlockSpec((d, tn), lambda i, j: (0, j)),
                  pl.BlockSpec((d, tn), lambda i, j: (0, j + nb)),
                  pl.BlockSpec((1, tn), lambda i, j: (0, j)),
                  pl.BlockSpec((1, tn), lambda i, j: (0, j + nb))],
        out_specs=pl.BlockSpec((tm, tn), lambda i, j: (i, j)),
        out_shape=jax.ShapeDtypeStruct((t, d), F32),
        compiler_params=_cparams("parallel", "parallel"),
        name="conv_glu",
    )(x, w_in, w_in, b2, b2)


def _dwconv_body(halo_ref, cur_ref, w_ref, b_ref, o_ref, buf_ref, shift_ref, *, width, ts):
    i = pl.program_id(1)
    halo = halo_ref[...]
    buf_ref[0:CONV_HALO, :] = jnp.where(i > 0, halo, jnp.zeros_like(halo))
    buf_ref[CONV_HALO:CONV_HALO + ts, :] = cur_ref[...]
    acc = jnp.broadcast_to(b_ref[...], o_ref.shape)
    first = CONV_HALO - (width - 1)
    for r in range(min(SUBLANES, width)):
        taps = range(r, width, SUBLANES)
        span = ts + (len(taps) - 1) * SUBLANES
        shift_ref[r, 0:span, :] = buf_ref[first + r:first + r + span, :]
        for a, k in enumerate(taps):
            acc = acc + w_ref[k:k + 1, :] * shift_ref[r, a * SUBLANES:a * SUBLANES + ts, :]
    o_ref[...] = acc


def _dwconv(g3, w_dw, b_dw, ts=512, td=256):
    b, s, d = g3.shape
    width = w_dw.shape[0]
    assert width <= CONV_HALO
    ts, td = min(ts, s), min(td, d)
    per = ts // CONV_HALO
    return pl.pallas_call(
        functools.partial(_dwconv_body, width=width, ts=ts),
        grid=(b, s // ts, d // td),
        in_specs=[pl.BlockSpec((None, CONV_HALO, td), lambda bi, i, j: (bi, jnp.maximum(i * per - 1, 0), j)),
                  pl.BlockSpec((None, ts, td), lambda bi, i, j: (bi, i, j)),
                  pl.BlockSpec((width, td), lambda bi, i, j: (0, j)),
                  pl.BlockSpec((1, td), lambda bi, i, j: (0, j))],
        out_specs=pl.BlockSpec((None, ts, td), lambda bi, i, j: (bi, i, j)),
        out_shape=jax.ShapeDtypeStruct((b, s, d), F32),
        scratch_shapes=[pltpu.VMEM((CONV_HALO + ts, td), F32),
                        pltpu.VMEM((SUBLANES, ts + CONV_HALO - SUBLANES, td), F32)],
        compiler_params=_cparams("parallel", "parallel", "parallel"),
        name="conv_dw",
    )(g3, g3, w_dw, b_dw.reshape(1, d))


def _proj_ln_body(*refs, alpha, pre_ln):
    if pre_ln:
        h_ref, x_ref, w_ref, pg_ref, pb_ref, bias_ref, g_ref, b_ref, o_ref, ob_ref, ot_ref = refs
        h = _layer_norm(h_ref[...], pg_ref[...], pb_ref[...])
        h = (h * jax.nn.sigmoid(h)).astype(BF16)
    else:
        h_ref, x_ref, w_ref, bias_ref, g_ref, b_ref, o_ref, ob_ref, ot_ref = refs
        h = h_ref[...]
    y = jnp.dot(h, w_ref[...], preferred_element_type=F32) + bias_ref[...]
    x1 = _layer_norm(alpha * x_ref[...] + y, g_ref[...], b_ref[...])
    o_ref[...] = x1
    ob_ref[...] = x1.astype(BF16)
    ot_ref[...] = x1.T.astype(BF16)


def _proj_ln(h, x, w, bias, g, b, alpha, pre=None, tm=256):
    t, d = x.shape
    tm = min(tm, t)
    row = lambda v: v.reshape(1, d)
    tok = pl.BlockSpec((tm, d), lambda i: (i, 0))
    vec = pl.BlockSpec((1, d), lambda i: (0, 0))
    mat = pl.BlockSpec((d, d), lambda i: (0, 0))
    if pre is not None:
        ins = [h, x, w, row(pre[0]), row(pre[1]), row(bias), row(g), row(b)]
        specs = [tok, tok, mat, vec, vec, vec, vec, vec]
    else:
        ins = [h, x, w, row(bias), row(g), row(b)]
        specs = [tok, tok, mat, vec, vec, vec]
    return pl.pallas_call(
        functools.partial(_proj_ln_body, alpha=alpha, pre_ln=pre is not None),
        grid=(t // tm,),
        in_specs=specs,
        out_specs=[tok, tok, pl.BlockSpec((d, tm), lambda i: (0, i))],
        out_shape=[jax.ShapeDtypeStruct((t, d), F32), jax.ShapeDtypeStruct((t, d), BF16),
                   jax.ShapeDtypeStruct((d, t), BF16)],
        compiler_params=_cparams("parallel"),
        name="proj_ln",
    )(*ins)


def _top_ranks(s, k, exact, want_rank=True):
    n, tm = s.shape
    krow = lax.broadcasted_iota(jnp.int32, (k, tm), 0)
    rank = jnp.full((n, tm), float(k), F32)
    vals = jnp.zeros((k, tm), F32)
    if exact:
        row = lax.broadcasted_iota(jnp.int32, (n, tm), 0).astype(F32)
    for p in range(k):
        m = jnp.max(s, axis=0, keepdims=True)
        hit = s == m
        if exact:
            first = jnp.min(jnp.where(hit, row, float(n)), axis=0, keepdims=True)
            hit = row == first
        if want_rank:
            rank = jnp.where(hit, float(p), rank)
        s = jnp.where(hit, -jnp.inf, s)
        vals = jnp.where(krow == p, m, vals)
    ranked = jnp.sum(jnp.where(rank < float(k), 1.0, 0.0), axis=0, keepdims=True) if want_rank else None
    return vals, rank, ranked


def _pair_counts(a, b, k, exact):
    tm = a.shape[1]
    half = k // 2
    p_full = lax.broadcasted_iota(jnp.int32, (k, tm), 0).astype(F32)
    p_half = lax.broadcasted_iota(jnp.int32, (half, tm), 0).astype(F32)
    cand = [a + b[0:1]]
    flat = [p_full * k]
    for q in range(1, k):
        lim = k // (q + 1)
        cand.append(jnp.where(p_half < lim, a[0:half] + b[q:q + 1], -jnp.inf))
        flat.append(p_half * k + q)
    cand = jnp.concatenate(cand, axis=0)
    flat = jnp.concatenate(flat, axis=0) if exact else None
    picked = jnp.zeros(cand.shape, F32)
    z = jnp.zeros((1, tm), F32)
    best = None
    for r in range(k):
        m = jnp.max(cand, axis=0, keepdims=True)
        best = m if r == 0 else best
        hit = cand == m
        if exact:
            first = jnp.min(jnp.where(hit, flat, float(k * k)), axis=0, keepdims=True)
            hit = flat == first
        picked = jnp.where(hit, 1.0, picked)
        cand = jnp.where(hit, -jnp.inf, cand)
        z = z + jnp.exp(m - best)
    count_lo = picked[0:half]
    for q in range(1, k):
        count_lo = count_lo + picked[k + (q - 1) * half:k + q * half]
    count = jnp.concatenate([count_lo, picked[half:k]], axis=0)
    n_picked = jnp.sum(count, axis=0, keepdims=True)
    return count, z, n_picked


def _route_tile_exact(s0, s1, k):
    a, rank0, _ = _top_ranks(s0, k, True)
    b, rank1, _ = _top_ranks(s1, k, True)
    count, z, _ = _pair_counts(a, b, k, True)
    n0 = jnp.zeros_like(s0)
    for p in range(k):
        n0 = jnp.where(rank0 == float(p), count[p:p + 1], n0)
    e1 = jnp.exp(s1 - b[0:1]) / z
    e0 = jnp.exp(s0 - a[0:1])
    return rank1.astype(BF16), e1.astype(BF16), n0, e0


def _quick_pairs(a, b):
    tm = a.shape[1]
    pf = lax.broadcasted_iota(jnp.int32, (16, tm), 0)
    ph = lax.broadcasted_iota(jnp.int32, (8, tm), 0)
    ninf = -jnp.inf
    return jnp.concatenate([
        b + a[0:1],
        b[0:8] + a[1:2],
        jnp.where(pf >= 2, a + b[0:1], ninf),
        jnp.where(ph >= 2, a[0:8] + b[1:2], ninf),
        jnp.where((ph >= 2) & (ph <= 4), b[0:8] + a[2:3], ninf),
        jnp.where((ph >= 2) & (ph <= 3), b[0:8] + a[3:4], ninf),
        jnp.where(ph == 2, b[0:8] + a[4:5], ninf)], axis=0)


def _route_tile_quick(s0, s1, k):
    assert k == 16
    tm = s0.shape[1]
    kf = float(k)
    a, _, _ = _top_ranks(s0, k, False, want_rank=False)
    b, rank1, ranked1 = _top_ranks(s1, k, False)
    cand = _quick_pairs(a, b)
    z = jnp.zeros((1, tm), F32)
    best = m = None
    for r in range(k):
        m = jnp.max(cand, axis=0, keepdims=True)
        best = m if r == 0 else best
        cand = jnp.where(cand == m, -jnp.inf, cand)
        z = z + jnp.exp(m - best)
    picked = jnp.where(_quick_pairs(a, b) >= m, 1.0, 0.0)
    ph = lax.broadcasted_iota(jnp.int32, (8, tm), 0)
    count_lo = picked[24:32] + picked[40:48]
    for p, rows in enumerate([(0, 16), (16, 24), (48, 56), (56, 64), (64, 72)]):
        count_lo = count_lo + jnp.where(ph == p, jnp.sum(picked[rows[0]:rows[1]], axis=0, keepdims=True), 0.0)
    count = jnp.concatenate([count_lo, picked[32:40]], axis=0)
    n_picked = jnp.sum(count, axis=0, keepdims=True)
    n0 = jnp.zeros_like(s0)
    for p in range(k):
        n0 = jnp.where(s0 == a[p:p + 1], count[p:p + 1], n0)
    ranked0 = jnp.sum(jnp.where(s0 >= a[k - 1:k], 1.0, 0.0), axis=0, keepdims=True)
    e1 = jnp.exp(s1 - b[0:1]) / z
    e0 = jnp.exp(s0 - a[0:1])
    doubt = jnp.where((ranked0 == kf) & (ranked1 == kf) & (n_picked == kf), 0.0, 1.0)
    return rank1.astype(BF16), e1.astype(BF16), n0, e0, doubt


def _pair_scores(q_ref, keys_ref, pair, n_tiles):
    dq = keys_ref.shape[3]
    dn = (((1,), (1,)), ((), ()))
    h = pair // n_tiles
    q = q_ref[h, pl.ds(pl.multiple_of((pair % n_tiles) * LANES, LANES), LANES), :]
    s0 = lax.dot_general(keys_ref[h, 0], q[:, :dq], dn, preferred_element_type=F32)
    s1 = lax.dot_general(keys_ref[h, 1], q[:, dq:], dn, preferred_element_type=F32)
    return s0, s1


def _route_pair(s0, s1, pair, n_tiles, k, outs):
    h = pair // n_tiles
    lanes = pl.ds(pl.multiple_of((pair % n_tiles) * LANES, LANES), LANES)

    def write(vals):
        for ref, val in zip(outs, vals):
            ref[h, :, lanes] = val

    *vals, doubt = _route_tile_quick(s0, s1, k)
    write(vals)

    def redo():
        write(_route_tile_exact(s0, s1, k))

    return redo, jnp.max(doubt)


def _peer_dense_body(q0_ref, qn_ref, keys_ref, xt_ref, u_ref, vt_ref, o_ref,
                     r1_s, e1_s, n0_s, e0_s, new_s, old_s, *, rows, k, group):
    i = pl.program_id(0)
    s = pl.program_id(1)
    n_exp = pl.num_programs(1) - 1
    _, n_heads, n_keys, tb = r1_s.shape
    n_tiles = tb // LANES
    n_pairs = n_heads * n_tiles
    cur = i % 2
    slots = lambda b: [r1_s.at[b], e1_s.at[b], n0_s.at[b], e0_s.at[b]]
    pair_of = lambda g, a: jnp.minimum(g * group + a, n_pairs - 1)

    def route_group(g):
        out = []
        for a in range(group):
            s0, s1 = _pair_scores(qn_ref, keys_ref, pair_of(g, a), n_tiles)
            out.append(_route_pair(s0, s1, pair_of(g, a), n_tiles, k, slots(1 - cur)))
        return out

    @pl.when(s == 0)
    def _():
        o_ref[...] = jnp.zeros_like(o_ref)
        new_s[...] = jnp.zeros_like(new_s)

    @pl.when((i == 0) & (s == 0))
    def _():
        def first(pair, carry):
            s0, s1 = _pair_scores(q0_ref, keys_ref, pair, n_tiles)
            redo, flag = _route_pair(s0, s1, pair, n_tiles, k, slots(0))
            pl.when(flag > 0.0)(redo)
            return carry
        lax.fori_loop(0, n_pairs, first, 0)

    @pl.when(s < n_exp)
    def _():
        old_s[...] = new_s[...]
        pending = route_group(s)
        ht = jnp.dot(u_ref[...], xt_ref[...], preferred_element_type=F32)
        for r in range(rows):
            w = jnp.zeros((n_keys, tb), BF16)
            for h in range(n_heads):
                n_row = jnp.broadcast_to(n0_s[cur, h, pl.ds(s * rows + r, 1), :], (n_keys, tb)).astype(BF16)
                e_row = jnp.broadcast_to(e0_s[cur, h, pl.ds(s * rows + r, 1), :], (n_keys, tb)).astype(BF16)
                w = w + jnp.where(r1_s[cur, h] < n_row, e1_s[cur, h], jnp.zeros((), BF16)) * e_row
            hr = ht[r * n_keys:(r + 1) * n_keys]
            act = 0.5 * hr * (1.0 + lax.erf(hr * (1.0 / math.sqrt(2.0))))
            new_s[r * n_keys:(r + 1) * n_keys, :] = act.astype(BF16) * w
        o_ref[...] += jnp.dot(vt_ref[...], old_s[...], preferred_element_type=F32)
        for redo, flag in pending:
            pl.when(flag > 0.0)(redo)

    @pl.when(s == n_exp)
    def _():
        o_ref[...] += jnp.dot(vt_ref[...], new_s[...], preferred_element_type=F32)


def _peer_dense(q3, keys, xt, u, vt, tb=512):
    d, t = xt.shape
    n_heads, _, n_keys, dq = keys.shape
    n_exp, _, eb = vt.shape
    rows = eb // n_keys
    tb = min(tb, t)
    assert tb % LANES == 0
    n_tok = t // tb
    group = -(-(n_heads * (tb // LANES)) // n_exp)
    slot_bf16 = pltpu.VMEM((2, n_heads, n_keys, tb), BF16)
    slot_f32 = pltpu.VMEM((2, n_heads, n_keys, tb), F32)
    coef = pltpu.VMEM((eb, tb), BF16)
    once = dict(pipeline_mode=pl.Buffered(1))
    return pl.pallas_call(
        functools.partial(_peer_dense_body, rows=rows, k=PEER_TOPK, group=group),
        grid=(n_tok, n_exp + 1),
        in_specs=[pl.BlockSpec((n_heads, tb, 2 * dq), lambda i, s: (0, 0, 0)),
                  pl.BlockSpec((n_heads, tb, 2 * dq), lambda i, s: (0, jnp.minimum(i + 1, n_tok - 1), 0), **once),
                  pl.BlockSpec((n_heads, 2, n_keys, dq), lambda i, s: (0, 0, 0, 0)),
                  pl.BlockSpec((d, tb), lambda i, s: (0, i), **once),
                  pl.BlockSpec((eb, d), lambda i, s: (jnp.minimum(s, n_exp - 1), 0)),
                  pl.BlockSpec((None, d, eb), lambda i, s: (jnp.maximum(s - 1, 0), 0, 0))],
        out_specs=pl.BlockSpec((d, tb), lambda i, s: (0, i)),
        out_shape=jax.ShapeDtypeStruct((d, t), F32),
        scratch_shapes=[slot_bf16, slot_bf16, slot_f32, slot_f32, coef, coef],
        compiler_params=_cparams("arbitrary", "arbitrary"),
        name="peer_dense",
    )(q3, q3, keys, xt, u, vt)


def _ln_t_body(x_ref, yt_ref, g_ref, b_ref, o_ref, ob_ref, *, alpha):
    x2 = _layer_norm(alpha * x_ref[...] + yt_ref[...].T, g_ref[...], b_ref[...])
    o_ref[...] = x2
    ob_ref[...] = x2.astype(BF16)


def _ln_t(x, yt, g, b, alpha, tm=256):
    t, d = x.shape
    tm = min(tm, t)
    tok = pl.BlockSpec((tm, d), lambda i: (i, 0))
    vec = pl.BlockSpec((1, d), lambda i: (0, 0))
    return pl.pallas_call(
        functools.partial(_ln_t_body, alpha=alpha),
        grid=(t // tm,),
        in_specs=[tok, pl.BlockSpec((d, tm), lambda i: (0, i)), vec, vec],
        out_specs=[tok, tok],
        out_shape=[jax.ShapeDtypeStruct((t, d), F32), jax.ShapeDtypeStruct((t, d), BF16)],
        compiler_params=_cparams("parallel"),
        name="peer_ln",
    )(x, yt, g.reshape(1, d), b.reshape(1, d))


def _peer(x, xb, xt, w_q, sub_keys, u_tabs, v_tabs, layer, g, b, alpha):
    n_keys, dq = sub_keys.shape[2], sub_keys.shape[3]
    q3 = _matmul_heads(xb, w_q.astype(BF16), 2 * dq, name="peer_q")
    u = _table_bf16(u_tabs, layer)
    vt = _table_blocks_t_bf16(v_tabs, layer, PEER_BLOCK_ROWS * n_keys)
    yt = _peer_dense(q3, sub_keys.astype(BF16), xt, u, vt)
    return _ln_t(x, yt, g, b, alpha)


def _sb_attn_body(q_ref, k_ref, v_ref, o_ref, acc_ref, run_ref, *, tq, dh, hp, scale):
    qi = pl.program_id(2)
    row = lax.broadcasted_iota(jnp.int32, (tq, tq), 0)
    col = lax.broadcasted_iota(jnp.int32, (tq, tq), 1)
    later = jnp.where(row > col, 1.0, 0.0).astype(BF16)
    causal = col < row

    def block(a, kb, mask, run):
        start = pl.multiple_of(kb * tq, tq)
        lanes = slice(a * dh, (a + 1) * dh)
        kblk = k_ref[pl.ds(start, tq), lanes]
        vblk = v_ref[pl.ds(start, tq), lanes]
        z = lax.dot_general(q_ref[:, lanes], kblk, (((1,), (1,)), ((), ())), preferred_element_type=F32) * scale
        log_b = jnp.minimum(z, 0.0) - jnp.log1p(jnp.exp(-jnp.abs(z)))
        log_1mb = log_b - z
        if mask is not None:
            log_1mb = jnp.where(mask, log_1mb, 0.0)
        hi = log_1mb.astype(BF16)
        lo = (log_1mb - hi.astype(F32)).astype(BF16)
        rest = (jnp.dot(hi, later, preferred_element_type=F32)
                + jnp.dot(lo, later, preferred_element_type=F32))
        att = jnp.exp(log_b + rest + run)
        if mask is not None:
            att = jnp.where(mask, att, 0.0)
        pv = jnp.dot(att.astype(BF16), vblk, preferred_element_type=F32)
        return pv, run + rest[:, 0:1] + log_1mb[:, 0:1]

    top = None
    for a in range(hp):
        pv0, run = block(a, qi, causal, jnp.zeros((tq, 1), F32))
        run = jnp.where(qi > 0, run, -jnp.inf)
        pv1, run = block(a, jnp.maximum(qi - 1, 0), None, run)
        acc_ref[a] = pv0 + pv1
        run_ref[a] = run
        top = jnp.max(run) if top is None else jnp.maximum(top, jnp.max(run))

    def cond(c):
        kb, top = c
        return (kb >= 0) & (top > SB_LOG_CUTOFF)

    def body(c):
        kb, _ = c
        top = None
        for a in range(hp):
            pv, run = block(a, kb, None, run_ref[a])
            acc_ref[a] += pv
            run_ref[a] = run
            top = jnp.max(run) if top is None else jnp.maximum(top, jnp.max(run))
        return kb - 1, top

    lax.while_loop(cond, body, (qi - 2, top))
    for a in range(hp):
        o_ref[:, a * dh:(a + 1) * dh] = acc_ref[a].astype(o_ref.dtype)


def _sb_attention(q3, kv3, n_heads, tq=256, hp=4):
    b, s, d = q3.shape
    dh = d // n_heads
    tq = min(tq, s)
    hp = min(hp, n_heads)
    groups = n_heads // hp
    return pl.pallas_call(
        functools.partial(_sb_attn_body, tq=tq, dh=dh, hp=hp, scale=dh ** -0.5),
        grid=(b, groups, s // tq),
        in_specs=[pl.BlockSpec((None, tq, hp * dh), lambda bi, h, i: (bi, i, h)),
                  pl.BlockSpec((None, s, hp * dh), lambda bi, h, i: (bi, 0, h)),
                  pl.BlockSpec((None, s, hp * dh), lambda bi, h, i: (bi, 0, h + groups))],
        out_specs=pl.BlockSpec((None, tq, hp * dh), lambda bi, h, i: (bi, i, h)),
        out_shape=jax.ShapeDtypeStruct((b, s, d), BF16),
        scratch_shapes=[pltpu.VMEM((hp, tq, dh), F32), pltpu.VMEM((hp, tq, 1), F32)],
        compiler_params=_cparams("parallel", "parallel", "arbitrary"),
        name="sb_attn",
    )(q3, kv3, kv3)


def kernel(x, conv_w_in, conv_b_in, conv_w_dw, conv_b_dw, conv_ln_g, conv_ln_b, conv_w_out, conv_b_out,
           attn_w_kv, attn_w_q, attn_w_o, peer_w_q, peer_sub_keys, peer_u, peer_v, ln_g, ln_b):
    bsz, seq, d = x.shape
    t = bsz * seq
    depth = peer_w_q.shape[0]
    n_a = conv_w_in.shape[0]
    alpha = (2.0 * depth) ** 0.25
    n_sb_heads = d // SB_HEAD_DIM

    xf = x.reshape(t, d)
    kv3 = None
    for i in range(depth):
        if i < n_a:
            g = _glu(xf, conv_w_in[i].astype(BF16), conv_b_in[i])
            c = _dwconv(g.reshape(bsz, seq, d), conv_w_dw[i].reshape(-1, d), conv_b_dw[i]).reshape(t, d)
            x1, x1b, x1t = _proj_ln(c, xf, conv_w_out[i].astype(BF16), conv_b_out[i], ln_g[i, 0], ln_b[i, 0],
                                    alpha, pre=(conv_ln_g[i], conv_ln_b[i]))
        else:
            j = i - n_a
            q = _matmul(xb, attn_w_q[j].astype(BF16), BF16, name="attn_q")
            o = _sb_attention(q.reshape(bsz, seq, d), kv3, n_sb_heads).reshape(t, d)
            x1, x1b, x1t = _proj_ln(o, xf, attn_w_o[j].astype(BF16), jnp.zeros((d,), F32), ln_g[i, 0],
                                    ln_b[i, 0], alpha)
        xf, xb = _peer(x1, x1b, x1t, peer_w_q[i], peer_sub_keys[i], peer_u, peer_v, i,
                       ln_g[i, 1], ln_b[i, 1], alpha)
        if i == n_a - 1:
            kv3 = _matmul(xb, attn_w_kv.astype(BF16), BF16, name="attn_kv").reshape(bsz, seq, 2 * d)
    return xf.reshape(bsz, seq, d)
```

```python
import functools
import math

import jax
import jax.numpy as jnp
from jax import lax
from jax.experimental import pallas as pl
from jax.experimental.pallas import tpu as pltpu

F32 = jnp.float32
BF16 = jnp.bfloat16

LN_EPS = 1e-5
LANES = 128
SUBLANES = 8
CONV_HALO = 32
PEER_TOPK = 16
PEER_BLOCK_ROWS = 8
SB_HEAD_DIM = 128
SB_LOG_CUTOFF = -104.0
VMEM_LIMIT_BYTES = 56 * 1024 * 1024


def _cparams(*sem):
    return pltpu.CompilerParams(dimension_semantics=sem, vmem_limit_bytes=VMEM_LIMIT_BYTES)


def _layer_norm(v, g, b):
    mu = jnp.mean(v, axis=-1, keepdims=True)
    c = v - mu
    var = jnp.mean(c * c, axis=-1, keepdims=True)
    return c * lax.rsqrt(var + LN_EPS) * g + b


def _mm_body(x_ref, w_ref, o_ref):
    o_ref[...] = jnp.dot(x_ref[...], w_ref[...], preferred_element_type=F32).astype(o_ref.dtype)


def _matmul(x, w, out_dtype, tm=1024, tn=1024, name="matmul"):
    m, k = x.shape
    n = w.shape[1]
    tm, tn = min(tm, m), min(tn, n)
    return pl.pallas_call(
        _mm_body,
        grid=(m // tm, n // tn),
        in_specs=[pl.BlockSpec((tm, k), lambda i, j: (i, 0)),
                  pl.BlockSpec((k, tn), lambda i, j: (0, j))],
        out_specs=pl.BlockSpec((tm, tn), lambda i, j: (i, j)),
        out_shape=jax.ShapeDtypeStruct((m, n), out_dtype),
        compiler_params=_cparams("parallel", "parallel"),
        name=name,
    )(x, w)


def _mm_heads_body(x_ref, w_ref, o_ref):
    hb, _, width = o_ref.shape
    res = jnp.dot(x_ref[...], w_ref[...], preferred_element_type=F32).astype(o_ref.dtype)
    for h in range(hb):
        o_ref[h] = res[:, h * width:(h + 1) * width]


def _matmul_heads(x, w, width, tm=1024, hb=4, name="matmul_heads"):
    m, k = x.shape
    n_heads = w.shape[1] // width
    tm = min(tm, m)
    hb = math.gcd(hb, n_heads)
    return pl.pallas_call(
        _mm_heads_body,
        grid=(m // tm, n_heads // hb),
        in_specs=[pl.BlockSpec((tm, k), lambda i, j: (i, 0)),
                  pl.BlockSpec((k, hb * width), lambda i, j: (0, j))],
        out_specs=pl.BlockSpec((hb, tm, width), lambda i, j: (j, i, 0)),
        out_shape=jax.ShapeDtypeStruct((n_heads, m, width), BF16),
        compiler_params=_cparams("parallel", "parallel"),
        name=name,
    )(x, w)


def _cast_body(x_ref, o_ref):
    o_ref[...] = x_ref[...].astype(o_ref.dtype)


def _cast_t_body(x_ref, o_ref):
    o_ref[...] = x_ref[...].T.astype(o_ref.dtype)


def _table_bf16(tab, layer, rows=1024):
    _, e, d = tab.shape
    rows = min(rows, e)
    return pl.pallas_call(
        _cast_body,
        grid=(e // rows,),
        in_specs=[pl.BlockSpec((None, rows, d), lambda j: (layer, j, 0))],
        out_specs=pl.BlockSpec((rows, d), lambda j: (j, 0)),
        out_shape=jax.ShapeDtypeStruct((e, d), BF16),
        compiler_params=_cparams("parallel"),
        name="table_bf16",
    )(tab)


def _table_blocks_t_bf16(tab, layer, eb, rows=512):
    _, e, d = tab.shape
    rows = min(rows, eb)
    per = eb // rows
    return pl.pallas_call(
        _cast_t_body,
        grid=(e // eb, per),
        in_specs=[pl.BlockSpec((None, rows, d), lambda j, c: (layer, j * per + c, 0))],
        out_specs=pl.BlockSpec((None, d, rows), lambda j, c: (j, 0, c)),
        out_shape=jax.ShapeDtypeStruct((e // eb, d, eb), BF16),
        compiler_params=_cparams("parallel", "parallel"),
        name="table_t_bf16",
    )(tab)


def _glu_body(x_ref, wa_ref, wg_ref, ba_ref, bg_ref, o_ref):
    xb = x_ref[...].astype(BF16)
    a = jnp.dot(xb, wa_ref[...], preferred_element_type=F32) + ba_ref[...]
    gate = jnp.dot(xb, wg_ref[...], preferred_element_type=F32) + bg_ref[...]
    o_ref[...] = a * jax.nn.sigmoid(gate)


def _glu(x, w_in, b_in, tm=1024, tn=512):
    t, d = x.shape
    tm, tn = min(tm, t), min(tn, d)
    nb = d // tn
    b2 = b_in.reshape(1, 2 * d)
    return pl.pallas_call(
        _glu_body,
        grid=(t // tm, nb),
        in_specs=[pl.BlockSpec((tm, d), lambda i, j: (i, 0)),
                  pl.BlockSpec((d, tn), lambda i, j: (0, j)),
                  pl.BlockSpec((d, tn), lambda i, j: (0, j + nb)),
                  pl.BlockSpec((1, tn), lambda i, j: (0, j)),
                  pl.BlockSpec((1, tn), lambda i, j: (0, j + nb))],
        out_specs=pl.BlockSpec((tm, tn), lambda i, j: (i, j)),
        out_shape=jax.ShapeDtypeStruct((t, d), F32),
        compiler_params=_cparams("parallel", "parallel"),
        name="conv_glu",
    )(x, w_in, w_in, b2, b2)


def _dwconv_body(halo_ref, cur_ref, w_ref, b_ref, o_ref, buf_ref, shift_ref, *, width, ts):
    i = pl.program_id(1)
    halo = halo_ref[...]
    buf_ref[0:CONV_HALO, :] = jnp.where(i > 0, halo, jnp.zeros_like(halo))
    buf_ref[CONV_HALO:CONV_HALO + ts, :] = cur_ref[...]
    acc = jnp.broadcast_to(b_ref[...], o_ref.shape)
    first = CONV_HALO - (width - 1)
    for r in range(min(SUBLANES, width)):
        taps = range(r, width, SUBLANES)
        span = ts + (len(taps) - 1) * SUBLANES
        shift_ref[r, 0:span, :] = buf_ref[first + r:first + r + span, :]
        for a, k in enumerate(taps):
            acc = acc + w_ref[k:k + 1, :] * shift_ref[r, a * SUBLANES:a * SUBLANES + ts, :]
    o_ref[...] = acc


def _dwconv(g3, w_dw, b_dw, ts=512, td=256):
    b, s, d = g3.shape
    width = w_dw.shape[0]
    assert width <= CONV_HALO
    ts, td = min(ts, s), min(td, d)
    per = ts // CONV_HALO
    return pl.pallas_call(
        functools.partial(_dwconv_body, width=width, ts=ts),
        grid=(b, s // ts, d // td),
        in_specs=[pl.BlockSpec((None, CONV_HALO, td), lambda bi, i, j: (bi, jnp.maximum(i * per - 1, 0), j)),
                  pl.BlockSpec((None, ts, td), lambda bi, i, j: (bi, i, j)),
                  pl.BlockSpec((width, td), lambda bi, i, j: (0, j)),
                  pl.BlockSpec((1, td), lambda bi, i, j: (0, j))],
        out_specs=pl.BlockSpec((None, ts, td), lambda bi, i, j: (bi, i, j)),
        out_shape=jax.ShapeDtypeStruct((b, s, d), F32),
        scratch_shapes=[pltpu.VMEM((CONV_HALO + ts, td), F32),
                        pltpu.VMEM((SUBLANES, ts + CONV_HALO - SUBLANES, td), F32)],
        compiler_params=_cparams("parallel", "parallel", "parallel"),
        name="conv_dw",
    )(g3, g3, w_dw, b_dw.reshape(1, d))


def _proj_ln_body(*refs, alpha, pre_ln):
    if pre_ln:
        h_ref, x_ref, w_ref, pg_ref, pb_ref, bias_ref, g_ref, b_ref, o_ref, ob_ref, ot_ref = refs
        h = _layer_norm(h_ref[...], pg_ref[...], pb_ref[...])
        h = (h * jax.nn.sigmoid(h)).astype(BF16)
    else:
        h_ref, x_ref, w_ref, bias_ref, g_ref, b_ref, o_ref, ob_ref, ot_ref = refs
        h = h_ref[...]
    y = jnp.dot(h, w_ref[...], preferred_element_type=F32) + bias_ref[...]
    x1 = _layer_norm(alpha * x_ref[...] + y, g_ref[...], b_ref[...])
    o_ref[...] = x1
    ob_ref[...] = x1.astype(BF16)
    ot_ref[...] = x1.T.astype(BF16)


def _proj_ln(h, x, w, bias, g, b, alpha, pre=None, tm=256):
    t, d = x.shape
    tm = min(tm, t)
    row = lambda v: v.reshape(1, d)
    tok = pl.BlockSpec((tm, d), lambda i: (i, 0))
    vec = pl.BlockSpec((1, d), lambda i: (0, 0))
    mat = pl.BlockSpec((d, d), lambda i: (0, 0))
    if pre is not None:
        ins = [h, x, w, row(pre[0]), row(pre[1]), row(bias), row(g), row(b)]
        specs = [tok, tok, mat, vec, vec, vec, vec, vec]
    else:
        ins = [h, x, w, row(bias), row(g), row(b)]
        specs = [tok, tok, mat, vec, vec, vec]
    return pl.pallas_call(
        functools.partial(_proj_ln_body, alpha=alpha, pre_ln=pre is not None),
        grid=(t // tm,),
        in_specs=specs,
        out_specs=[tok, tok, pl.BlockSpec((d, tm), lambda i: (0, i))],
        out_shape=[jax.ShapeDtypeStruct((t, d), F32), jax.ShapeDtypeStruct((t, d), BF16),
                   jax.ShapeDtypeStruct((d, t), BF16)],
        compiler_params=_cparams("parallel"),
        name="proj_ln",
    )(*ins)


def _top_ranks(s, k, exact, want_rank=True):
    n, tm = s.shape
    krow = lax.broadcasted_iota(jnp.int32, (k, tm), 0)
    rank = jnp.full((n, tm), float(k), F32)
    vals = jnp.zeros((k, tm), F32)
    if exact:
        row = lax.broadcasted_iota(jnp.int32, (n, tm), 0).astype(F32)
    for p in range(k):
        m = jnp.max(s, axis=0, keepdims=True)
        hit = s == m
        if exact:
            first = jnp.min(jnp.where(hit, row, float(n)), axis=0, keepdims=True)
            hit = row == first
        if want_rank:
            rank = jnp.where(hit, float(p), rank)
        s = jnp.where(hit, -jnp.inf, s)
        vals = jnp.where(krow == p, m, vals)
    ranked = jnp.sum(jnp.where(rank < float(k), 1.0, 0.0), axis=0, keepdims=True) if want_rank else None
    return vals, rank, ranked


def _pair_counts(a, b, k, exact):
    tm = a.shape[1]
    half = k // 2
    p_full = lax.broadcasted_iota(jnp.int32, (k, tm), 0).astype(F32)
    p_half = lax.broadcasted_iota(jnp.int32, (half, tm), 0).astype(F32)
    cand = [a + b[0:1]]
    flat = [p_full * k]
    for q in range(1, k):
        lim = k // (q + 1)
        cand.append(jnp.where(p_half < lim, a[0:half] + b[q:q + 1], -jnp.inf))
        flat.append(p_half * k + q)
    cand = jnp.concatenate(cand, axis=0)
    flat = jnp.concatenate(flat, axis=0) if exact else None
    picked = jnp.zeros(cand.shape, F32)
    z = jnp.zeros((1, tm), F32)
    best = None
    for r in range(k):
        m = jnp.max(cand, axis=0, keepdims=True)
        best = m if r == 0 else best
        hit = cand == m
        if exact:
            first = jnp.min(jnp.where(hit, flat, float(k * k)), axis=0, keepdims=True)
            hit = flat == first
        picked = jnp.where(hit, 1.0, picked)
        cand = jnp.where(hit, -jnp.inf, cand)
        z = z + jnp.exp(m - best)
    count_lo = picked[0:half]
    for q in range(1, k):
        count_lo = count_lo + picked[k + (q - 1) * half:k + q * half]
    count = jnp.concatenate([count_lo, picked[half:k]], axis=0)
    n_picked = jnp.sum(count, axis=0, keepdims=True)
    return count, z, n_picked


def _route_tile_exact(s0, s1, k):
    a, rank0, _ = _top_ranks(s0, k, True)
    b, rank1, _ = _top_ranks(s1, k, True)
    count, z, _ = _pair_counts(a, b, k, True)
    n0 = jnp.zeros_like(s0)
    for p in range(k):
        n0 = jnp.where(rank0 == float(p), count[p:p + 1], n0)
    e1 = jnp.exp(s1 - b[0:1]) / z
    e0 = jnp.exp(s0 - a[0:1])
    return rank1.astype(BF16), e1.astype(BF16), n0, e0


def _quick_pairs(a, b):
    tm = a.shape[1]
    pf = lax.broadcasted_iota(jnp.int32, (16, tm), 0)
    ph = lax.broadcasted_iota(jnp.int32, (8, tm), 0)
    ninf = -jnp.inf
    return jnp.concatenate([
        b + a[0:1],
        b[0:8] + a[1:2],
        jnp.where(pf >= 2, a + b[0:1], ninf),
        jnp.where(ph >= 2, a[0:8] + b[1:2], ninf),
        jnp.where((ph >= 2) & (ph <= 4), b[0:8] + a[2:3], ninf),
        jnp.where((ph >= 2) & (ph <= 3), b[0:8] + a[3:4], ninf),
        jnp.where(ph == 2, b[0:8] + a[4:5], ninf)], axis=0)


def _route_tile_quick(s0, s1, k):
    assert k == 16
    tm = s0.shape[1]
    kf = float(k)
    a, _, _ = _top_ranks(s0, k, False, want_rank=False)
    b, rank1, ranked1 = _top_ranks(s1, k, False)
    cand = _quick_pairs(a, b)
    z = jnp.zeros((1, tm), F32)
    best = m = None
    for r in range(k):
        m = jnp.max(cand, axis=0, keepdims=True)
        best = m if r == 0 else best
        cand = jnp.where(cand == m, -jnp.inf, cand)
        z = z + jnp.exp(m - best)
    picked = jnp.where(_quick_pairs(a, b) >= m, 1.0, 0.0)
    ph = lax.broadcasted_iota(jnp.int32, (8, tm), 0)
    count_lo = picked[24:32] + picked[40:48]
    for p, rows in enumerate([(0, 16), (16, 24), (48, 56), (56, 64), (64, 72)]):
        count_lo = count_lo + jnp.where(ph == p, jnp.sum(picked[rows[0]:rows[1]], axis=0, keepdims=True), 0.0)
    count = jnp.concatenate([count_lo, picked[32:40]], axis=0)
    n_picked = jnp.sum(count, axis=0, keepdims=True)
    n0 = jnp.zeros_like(s0)
    for p in range(k):
        n0 = jnp.where(s0 == a[p:p + 1], count[p:p + 1], n0)
    ranked0 = jnp.sum(jnp.where(s0 >= a[k - 1:k], 1.0, 0.0), axis=0, keepdims=True)
    e1 = jnp.exp(s1 - b[0:1]) / z
    e0 = jnp.exp(s0 - a[0:1])
    doubt = jnp.where((ranked0 == kf) & (ranked1 == kf) & (n_picked == kf), 0.0, 1.0)
    return rank1.astype(BF16), e1.astype(BF16), n0, e0, doubt


def _pair_scores(q_ref, keys_ref, pair, n_tiles):
    dq = keys_ref.shape[3]
    dn = (((1,), (1,)), ((), ()))
    h = pair // n_tiles
    q = q_ref[h, pl.ds(pl.multiple_of((pair % n_tiles) * LANES, LANES), LANES), :]
    s0 = lax.dot_general(keys_ref[h, 0], q[:, :dq], dn, preferred_element_type=F32)
    s1 = lax.dot_general(keys_ref[h, 1], q[:, dq:], dn, preferred_element_type=F32)
    return s0, s1


def _route_pair(s0, s1, pair, n_tiles, k, outs):
    h = pair // n_tiles
    lanes = pl.ds(pl.multiple_of((pair % n_tiles) * LANES, LANES), LANES)

    def write(vals):
        for ref, val in zip(outs, vals):
            ref[h, :, lanes] = val

    *vals, doubt = _route_tile_quick(s0, s1, k)
    write(vals)

    def redo():
        write(_route_tile_exact(s0, s1, k))

    return redo, jnp.max(doubt)


def _peer_dense_body(q0_ref, qn_ref, keys_ref, xt_ref, u_ref, vt_ref, o_ref, r1_s, e1_s, n0_s, e0_s,
                     *, rows, k, group):
    i = pl.program_id(0)
    s = pl.program_id(1)
    _, n_heads, n_keys, tb = r1_s.shape
    n_tiles = tb // LANES
    n_pairs = n_heads * n_tiles
    cur = i % 2
    slots = lambda b: [r1_s.at[b], e1_s.at[b], n0_s.at[b], e0_s.at[b]]
    pair_of = lambda a: jnp.minimum(s * group + a, n_pairs - 1)

    @pl.when(s == 0)
    def _():
        o_ref[...] = jnp.zeros_like(o_ref)

    @pl.when((i == 0) & (s == 0))
    def _():
        def first(pair, carry):
            s0, s1 = _pair_scores(q0_ref, keys_ref, pair, n_tiles)
            redo, flag = _route_pair(s0, s1, pair, n_tiles, k, slots(0))
            pl.when(flag > 0.0)(redo)
            return carry
        lax.fori_loop(0, n_pairs, first, 0)

    pending = []
    for a in range(group):
        s0, s1 = _pair_scores(qn_ref, keys_ref, pair_of(a), n_tiles)
        pending.append(_route_pair(s0, s1, pair_of(a), n_tiles, k, slots(1 - cur)))

    ht = jnp.dot(u_ref[...], xt_ref[...], preferred_element_type=F32)
    coef = []
    for r in range(rows):
        w = jnp.zeros((n_keys, tb), BF16)
        for h in range(n_heads):
            n_row = jnp.broadcast_to(n0_s[cur, h, pl.ds(s * rows + r, 1), :], (n_keys, tb)).astype(BF16)
            e_row = jnp.broadcast_to(e0_s[cur, h, pl.ds(s * rows + r, 1), :], (n_keys, tb)).astype(BF16)
            w = w + jnp.where(r1_s[cur, h] < n_row, e1_s[cur, h], jnp.zeros((), BF16)) * e_row
        hr = ht[r * n_keys:(r + 1) * n_keys]
        act = 0.5 * hr * (1.0 + lax.erf(hr * (1.0 / math.sqrt(2.0))))
        coef.append(act.astype(BF16) * w)
    coef = jnp.concatenate(coef, axis=0)
    o_ref[...] += jnp.dot(vt_ref[...], coef, preferred_element_type=F32)

    for redo, flag in pending:
        pl.when(flag > 0.0)(redo)


def _peer_dense(q3, keys, xt, u, vt, tb=512):
    d, t = xt.shape
    n_heads, _, n_keys, dq = keys.shape
    n_exp, _, eb = vt.shape
    rows = eb // n_keys
    tb = min(tb, t)
    assert tb % LANES == 0
    n_tok = t // tb
    group = -(-(n_heads * (tb // LANES)) // n_exp)
    slot_bf16 = pltpu.VMEM((2, n_heads, n_keys, tb), BF16)
    slot_f32 = pltpu.VMEM((2, n_heads, n_keys, tb), F32)
    return pl.pallas_call(
        functools.partial(_peer_dense_body, rows=rows, k=PEER_TOPK, group=group),
        grid=(n_tok, n_exp),
        in_specs=[pl.BlockSpec((n_heads, tb, 2 * dq), lambda i, s: (0, 0, 0)),
                  pl.BlockSpec((n_heads, tb, 2 * dq), lambda i, s: (0, jnp.minimum(i + 1, n_tok - 1), 0)),
                  pl.BlockSpec((n_heads, 2, n_keys, dq), lambda i, s: (0, 0, 0, 0)),
                  pl.BlockSpec((d, tb), lambda i, s: (0, i)),
                  pl.BlockSpec((eb, d), lambda i, s: (s, 0)),
                  pl.BlockSpec((None, d, eb), lambda i, s: (s, 0, 0))],
        out_specs=pl.BlockSpec((d, tb), lambda i, s: (0, i)),
        out_shape=jax.ShapeDtypeStruct((d, t), F32),
        scratch_shapes=[slot_bf16, slot_bf16, slot_f32, slot_f32],
        compiler_params=_cparams("arbitrary", "arbitrary"),
        name="peer_dense",
    )(q3, q3, keys, xt, u, vt)


def _ln_t_body(x_ref, yt_ref, g_ref, b_ref, o_ref, ob_ref, *, alpha):
    x2 = _layer_norm(alpha * x_ref[...] + yt_ref[...].T, g_ref[...], b_ref[...])
    o_ref[...] = x2
    ob_ref[...] = x2.astype(BF16)


def _ln_t(x, yt, g, b, alpha, tm=256):
    t, d = x.shape
    tm = min(tm, t)
    tok = pl.BlockSpec((tm, d), lambda i: (i, 0))
    vec = pl.BlockSpec((1, d), lambda i: (0, 0))
    return pl.pallas_call(
        functools.partial(_ln_t_body, alpha=alpha),
        grid=(t // tm,),
        in_specs=[tok, pl.BlockSpec((d, tm), lambda i: (0, i)), vec, vec],
        out_specs=[tok, tok],
        out_shape=[jax.ShapeDtypeStruct((t, d), F32), jax.ShapeDtypeStruct((t, d), BF16)],
        compiler_params=_cparams("parallel"),
        name="peer_ln",
    )(x, yt, g.reshape(1, d), b.reshape(1, d))


def _peer(x, xb, xt, w_q, sub_keys, u_tabs, v_tabs, layer, g, b, alpha):
    n_keys, dq = sub_keys.shape[2], sub_keys.shape[3]
    q3 = _matmul_heads(xb, w_q.astype(BF16), 2 * dq, name="peer_q")
    u = _table_bf16(u_tabs, layer)
    vt = _table_blocks_t_bf16(v_tabs, layer, PEER_BLOCK_ROWS * n_keys)
    yt = _peer_dense(q3, sub_keys.astype(BF16), xt, u, vt)
    return _ln_t(x, yt, g, b, alpha)


def _sb_attn_body(q_ref, k_ref, v_ref, o_ref, acc_ref, run_ref, *, tq, dh, hp, scale):
    qi = pl.program_id(2)
    row = lax.broadcasted_iota(jnp.int32, (tq, tq), 0)
    col = lax.broadcasted_iota(jnp.int32, (tq, tq), 1)
    later = jnp.where(row > col, 1.0, 0.0).astype(BF16)
    causal = col < row

    def block(a, kb, mask, run):
        start = pl.multiple_of(kb * tq, tq)
        lanes = slice(a * dh, (a + 1) * dh)
        kblk = k_ref[pl.ds(start, tq), lanes]
        vblk = v_ref[pl.ds(start, tq), lanes]
        z = lax.dot_general(q_ref[:, lanes], kblk, (((1,), (1,)), ((), ())), preferred_element_type=F32) * scale
        log_b = jnp.minimum(z, 0.0) - jnp.log(1.0 + jnp.exp(-jnp.abs(z)))
        log_1mb = log_b - z
        if mask is not None:
            log_1mb = jnp.where(mask, log_1mb, 0.0)
        hi = log_1mb.astype(BF16)
        lo = (log_1mb - hi.astype(F32)).astype(BF16)
        rest = (jnp.dot(hi, later, preferred_element_type=F32)
                + jnp.dot(lo, later, preferred_element_type=F32))
        att = jnp.exp(log_b + rest + run)
        if mask is not None:
            att = jnp.where(mask, att, 0.0)
        pv = jnp.dot(att.astype(BF16), vblk, preferred_element_type=F32)
        return pv, run + rest[:, 0:1] + log_1mb[:, 0:1]

    top = None
    for a in range(hp):
        pv0, run = block(a, qi, causal, jnp.zeros((tq, 1), F32))
        run = jnp.where(qi > 0, run, -jnp.inf)
        pv1, run = block(a, jnp.maximum(qi - 1, 0), None, run)
        acc_ref[a] = pv0 + pv1
        run_ref[a] = run
        top = jnp.max(run) if top is None else jnp.maximum(top, jnp.max(run))

    def cond(c):
        kb, top = c
        return (kb >= 0) & (top > SB_LOG_CUTOFF)

    def body(c):
        kb, _ = c
        top = None
        for a in range(hp):
            pv, run = block(a, kb, None, run_ref[a])
            acc_ref[a] += pv
            run_ref[a] = run
            top = jnp.max(run) if top is None else jnp.maximum(top, jnp.max(run))
        return kb - 1, top

    lax.while_loop(cond, body, (qi - 2, top))
    for a in range(hp):
        o_ref[:, a * dh:(a + 1) * dh] = acc_ref[a].astype(o_ref.dtype)


def _sb_attention(q3, kv3, n_heads, tq=256, hp=4):
    b, s, d = q3.shape
    dh = d // n_heads
    tq = min(tq, s)
    hp = min(hp, n_heads)
    groups = n_heads // hp
    return pl.pallas_call(
        functools.partial(_sb_attn_body, tq=tq, dh=dh, hp=hp, scale=dh ** -0.5),
        grid=(b, groups, s // tq),
        in_specs=[pl.BlockSpec((None, tq, hp * dh), lambda bi, h, i: (bi, i, h)),
                  pl.BlockSpec((None, s, hp * dh), lambda bi, h, i: (bi, 0, h)),
                  pl.BlockSpec((None, s, hp * dh), lambda bi, h, i: (bi, 0, h + groups))],
        out_specs=pl.BlockSpec((None, tq, hp * dh), lambda bi, h, i: (bi, i, h)),
        out_shape=jax.ShapeDtypeStruct((b, s, d), BF16),
        scratch_shapes=[pltpu.VMEM((hp, tq, dh), F32), pltpu.VMEM((hp, tq, 1), F32)],
        compiler_params=_cparams("parallel", "parallel", "arbitrary"),
        name="sb_attn",
    )(q3, kv3, kv3)


def kernel(x, conv_w_in, conv_b_in, conv_w_dw, conv_b_dw, conv_ln_g, conv_ln_b, conv_w_out, conv_b_out,
           attn_w_kv, attn_w_q, attn_w_o, peer_w_q, peer_sub_keys, peer_u, peer_v, ln_g, ln_b):
    bsz, seq, d = x.shape
    t = bsz * seq
    depth = peer_w_q.shape[0]
    n_a = conv_w_in.shape[0]
    alpha = (2.0 * depth) ** 0.25
    n_sb_heads = d // SB_HEAD_DIM

    xf = x.reshape(t, d)
    kv3 = None
    for i in range(depth):
        if i < n_a:
            g = _glu(xf, conv_w_in[i].astype(BF16), conv_b_in[i])
            c = _dwconv(g.reshape(bsz, seq, d), conv_w_dw[i].reshape(-1, d), conv_b_dw[i]).reshape(t, d)
            x1, x1b, x1t = _proj_ln(c, xf, conv_w_out[i].astype(BF16), conv_b_out[i], ln_g[i, 0], ln_b[i, 0],
                                    alpha, pre=(conv_ln_g[i], conv_ln_b[i]))
        else:
            j = i - n_a
            q = _matmul(xb, attn_w_q[j].astype(BF16), BF16, name="attn_q")
            o = _sb_attention(q.reshape(bsz, seq, d), kv3, n_sb_heads).reshape(t, d)
            x1, x1b, x1t = _proj_ln(o, xf, attn_w_o[j].astype(BF16), jnp.zeros((d,), F32), ln_g[i, 0],
                                    ln_b[i, 0], alpha)
        xf, xb = _peer(x1, x1b, x1t, peer_w_q[i], peer_sub_keys[i], peer_u, peer_v, i,
                       ln_g[i, 1], ln_b[i, 1], alpha)
        if i == n_a - 1:
            kv3 = _matmul(xb, attn_w_kv.astype(BF16), BF16, name="attn_kv").reshape(bsz, seq, 2 * d)
    return xf.reshape(bsz, seq, d)
```

```python
import functools
import math

import jax
import jax.numpy as jnp
from jax import lax
from jax.experimental import pallas as pl
from jax.experimental.pallas import tpu as pltpu

F32 = jnp.float32
BF16 = jnp.bfloat16

LN_EPS = 1e-5
LANES = 128
SUBLANES = 8
CONV_HALO = 32
PEER_TOPK = 16
PEER_BLOCK_ROWS = 8
SB_HEAD_DIM = 128
SB_LOG_CUTOFF = -104.0
VMEM_LIMIT_BYTES = 56 * 1024 * 1024


def _cparams(*sem):
    return pltpu.CompilerParams(dimension_semantics=sem, vmem_limit_bytes=VMEM_LIMIT_BYTES)


def _layer_norm(v, g, b):
    mu = jnp.mean(v, axis=-1, keepdims=True)
    c = v - mu
    var = jnp.mean(c * c, axis=-1, keepdims=True)
    return c * lax.rsqrt(var + LN_EPS) * g + b


def _mm_body(x_ref, w_ref, o_ref):
    o_ref[...] = jnp.dot(x_ref[...], w_ref[...], preferred_element_type=F32).astype(o_ref.dtype)


def _matmul(x, w, out_dtype, tm=1024, tn=1024, name="matmul"):
    m, k = x.shape
    n = w.shape[1]
    tm, tn = min(tm, m), min(tn, n)
    return pl.pallas_call(
        _mm_body,
        grid=(m // tm, n // tn),
        in_specs=[pl.BlockSpec((tm, k), lambda i, j: (i, 0)),
                  pl.BlockSpec((k, tn), lambda i, j: (0, j))],
        out_specs=pl.BlockSpec((tm, tn), lambda i, j: (i, j)),
        out_shape=jax.ShapeDtypeStruct((m, n), out_dtype),
        compiler_params=_cparams("parallel", "parallel"),
        name=name,
    )(x, w)


def _mm_heads_body(x_ref, w_ref, o_ref):
    hb, _, width = o_ref.shape
    res = jnp.dot(x_ref[...], w_ref[...], preferred_element_type=F32).astype(o_ref.dtype)
    for h in range(hb):
        o_ref[h] = res[:, h * width:(h + 1) * width]


def _matmul_heads(x, w, width, tm=1024, hb=4, name="matmul_heads"):
    m, k = x.shape
    n_heads = w.shape[1] // width
    tm = min(tm, m)
    hb = math.gcd(hb, n_heads)
    return pl.pallas_call(
        _mm_heads_body,
        grid=(m // tm, n_heads // hb),
        in_specs=[pl.BlockSpec((tm, k), lambda i, j: (i, 0)),
                  pl.BlockSpec((k, hb * width), lambda i, j: (0, j))],
        out_specs=pl.BlockSpec((hb, tm, width), lambda i, j: (j, i, 0)),
        out_shape=jax.ShapeDtypeStruct((n_heads, m, width), BF16),
        compiler_params=_cparams("parallel", "parallel"),
        name=name,
    )(x, w)


def _cast_body(x_ref, o_ref):
    o_ref[...] = x_ref[...].astype(o_ref.dtype)


def _cast_t_body(x_ref, o_ref):
    o_ref[...] = x_ref[...].T.astype(o_ref.dtype)


def _table_bf16(tab, layer, rows=1024):
    _, e, d = tab.shape
    rows = min(rows, e)
    return pl.pallas_call(
        _cast_body,
        grid=(e // rows,),
        in_specs=[pl.BlockSpec((None, rows, d), lambda j: (layer, j, 0))],
        out_specs=pl.BlockSpec((rows, d), lambda j: (j, 0)),
        out_shape=jax.ShapeDtypeStruct((e, d), BF16),
        compiler_params=_cparams("parallel"),
        name="table_bf16",
    )(tab)


def _table_blocks_t_bf16(tab, layer, eb, rows=512):
    _, e, d = tab.shape
    rows = min(rows, eb)
    per = eb // rows
    return pl.pallas_call(
        _cast_t_body,
        grid=(e // eb, per),
        in_specs=[pl.BlockSpec((None, rows, d), lambda j, c: (layer, j * per + c, 0))],
        out_specs=pl.BlockSpec((None, d, rows), lambda j, c: (j, 0, c)),
        out_shape=jax.ShapeDtypeStruct((e // eb, d, eb), BF16),
        compiler_params=_cparams("parallel", "parallel"),
        name="table_t_bf16",
    )(tab)


def _glu_body(x_ref, wa_ref, wg_ref, ba_ref, bg_ref, o_ref):
    xb = x_ref[...].astype(BF16)
    a = jnp.dot(xb, wa_ref[...], preferred_element_type=F32) + ba_ref[...]
    gate = jnp.dot(xb, wg_ref[...], preferred_element_type=F32) + bg_ref[...]
    o_ref[...] = a * jax.nn.sigmoid(gate)


def _glu(x, w_in, b_in, tm=1024, tn=512):
    t, d = x.shape
    tm, tn = min(tm, t), min(tn, d)
    nb = d // tn
    b2 = b_in.reshape(1, 2 * d)
    return pl.pallas_call(
        _glu_body,
        grid=(t // tm, nb),
        in_specs=[pl.BlockSpec((tm, d), lambda i, j: (i, 0)),
                  pl.BlockSpec((d, tn), lambda i, j: (0, j)),
                  pl.BlockSpec((d, tn), lambda i, j: (0, j + nb)),
                  pl.BlockSpec((1, tn), lambda i, j: (0, j)),
                  pl.BlockSpec((1, tn), lambda i, j: (0, j + nb))],
        out_specs=pl.BlockSpec((tm, tn), lambda i, j: (i, j)),
        out_shape=jax.ShapeDtypeStruct((t, d), F32),
        compiler_params=_cparams("parallel", "parallel"),
        name="conv_glu",
    )(x, w_in, w_in, b2, b2)


def _dwconv_body(halo_ref, cur_ref, w_ref, b_ref, o_ref, buf_ref, shift_ref, *, width, ts):
    i = pl.program_id(1)
    halo = halo_ref[...]
    buf_ref[0:CONV_HALO, :] = jnp.where(i > 0, halo, jnp.zeros_like(halo))
    buf_ref[CONV_HALO:CONV_HALO + ts, :] = cur_ref[...]
    acc = jnp.broadcast_to(b_ref[...], o_ref.shape)
    first = CONV_HALO - (width - 1)
    for r in range(min(SUBLANES, width)):
        taps = range(r, width, SUBLANES)
        span = ts + (len(taps) - 1) * SUBLANES
        shift_ref[r, 0:span, :] = buf_ref[first + r:first + r + span, :]
        for a, k in enumerate(taps):
            acc = acc + w_ref[k:k + 1, :] * shift_ref[r, a * SUBLANES:a * SUBLANES + ts, :]
    o_ref[...] = acc


def _dwconv(g3, w_dw, b_dw, ts=512, td=256):
    b, s, d = g3.shape
    width = w_dw.shape[0]
    assert width <= CONV_HALO
    ts, td = min(ts, s), min(td, d)
    per = ts // CONV_HALO
    return pl.pallas_call(
        functools.partial(_dwconv_body, width=width, ts=ts),
        grid=(b, s // ts, d // td),
        in_specs=[pl.BlockSpec((None, CONV_HALO, td), lambda bi, i, j: (bi, jnp.maximum(i * per - 1, 0), j)),
                  pl.BlockSpec((None, ts, td), lambda bi, i, j: (bi, i, j)),
                  pl.BlockSpec((width, td), lambda bi, i, j: (0, j)),
                  pl.BlockSpec((1, td), lambda bi, i, j: (0, j))],
        out_specs=pl.BlockSpec((None, ts, td), lambda bi, i, j: (bi, i, j)),
        out_shape=jax.ShapeDtypeStruct((b, s, d), F32),
        scratch_shapes=[pltpu.VMEM((CONV_HALO + ts, td), F32),
                        pltpu.VMEM((SUBLANES, ts + CONV_HALO - SUBLANES, td), F32)],
        compiler_params=_cparams("parallel", "parallel", "parallel"),
        name="conv_dw",
    )(g3, g3, w_dw, b_dw.reshape(1, d))


def _proj_ln_body(*refs, alpha, pre_ln):
    if pre_ln:
        h_ref, x_ref, w_ref, pg_ref, pb_ref, bias_ref, g_ref, b_ref, o_ref, ob_ref, ot_ref = refs
        h = _layer_norm(h_ref[...], pg_ref[...], pb_ref[...])
        h = (h * jax.nn.sigmoid(h)).astype(BF16)
    else:
        h_ref, x_ref, w_ref, bias_ref, g_ref, b_ref, o_ref, ob_ref, ot_ref = refs
        h = h_ref[...]
    y = jnp.dot(h, w_ref[...], preferred_element_type=F32) + bias_ref[...]
    x1 = _layer_norm(alpha * x_ref[...] + y, g_ref[...], b_ref[...])
    o_ref[...] = x1
    ob_ref[...] = x1.astype(BF16)
    ot_ref[...] = x1.T.astype(BF16)


def _proj_ln(h, x, w, bias, g, b, alpha, pre=None, tm=512):
    t, d = x.shape
    tm = min(tm, t)
    row = lambda v: v.reshape(1, d)
    tok = pl.BlockSpec((tm, d), lambda i: (i, 0))
    vec = pl.BlockSpec((1, d), lambda i: (0, 0))
    mat = pl.BlockSpec((d, d), lambda i: (0, 0))
    if pre is not None:
        ins = [h, x, w, row(pre[0]), row(pre[1]), row(bias), row(g), row(b)]
        specs = [tok, tok, mat, vec, vec, vec, vec, vec]
    else:
        ins = [h, x, w, row(bias), row(g), row(b)]
        specs = [tok, tok, mat, vec, vec, vec]
    return pl.pallas_call(
        functools.partial(_proj_ln_body, alpha=alpha, pre_ln=pre is not None),
        grid=(t // tm,),
        in_specs=specs,
        out_specs=[tok, tok, pl.BlockSpec((d, tm), lambda i: (0, i))],
        out_shape=[jax.ShapeDtypeStruct((t, d), F32), jax.ShapeDtypeStruct((t, d), BF16),
                   jax.ShapeDtypeStruct((d, t), BF16)],
        compiler_params=_cparams("parallel"),
        name="proj_ln",
    )(*ins)


def _top_ranks(s, k, exact, want_rank=True):
    n, tm = s.shape
    krow = lax.broadcasted_iota(jnp.int32, (k, tm), 0)
    rank = jnp.full((n, tm), float(k), F32)
    vals = jnp.zeros((k, tm), F32)
    if exact:
        row = lax.broadcasted_iota(jnp.int32, (n, tm), 0).astype(F32)
    for p in range(k):
        m = jnp.max(s, axis=0, keepdims=True)
        hit = s == m
        if exact:
            first = jnp.min(jnp.where(hit, row, float(n)), axis=0, keepdims=True)
            hit = row == first
        if want_rank:
            rank = jnp.where(hit, float(p), rank)
        s = jnp.where(hit, -jnp.inf, s)
        vals = jnp.where(krow == p, m, vals)
    ranked = jnp.sum(jnp.where(rank < float(k), 1.0, 0.0), axis=0, keepdims=True) if want_rank else None
    return vals, rank, ranked


def _pair_counts(a, b, k, exact):
    tm = a.shape[1]
    half = k // 2
    p_full = lax.broadcasted_iota(jnp.int32, (k, tm), 0).astype(F32)
    p_half = lax.broadcasted_iota(jnp.int32, (half, tm), 0).astype(F32)
    cand = [a + b[0:1]]
    flat = [p_full * k]
    for q in range(1, k):
        lim = k // (q + 1)
        cand.append(jnp.where(p_half < lim, a[0:half] + b[q:q + 1], -jnp.inf))
        flat.append(p_half * k + q)
    cand = jnp.concatenate(cand, axis=0)
    flat = jnp.concatenate(flat, axis=0) if exact else None
    picked = jnp.zeros(cand.shape, F32)
    z = jnp.zeros((1, tm), F32)
    best = None
    for r in range(k):
        m = jnp.max(cand, axis=0, keepdims=True)
        best = m if r == 0 else best
        hit = cand == m
        if exact:
            first = jnp.min(jnp.where(hit, flat, float(k * k)), axis=0, keepdims=True)
            hit = flat == first
        picked = jnp.where(hit, 1.0, picked)
        cand = jnp.where(hit, -jnp.inf, cand)
        z = z + jnp.exp(m - best)
    count_lo = picked[0:half]
    for q in range(1, k):
        count_lo = count_lo + picked[k + (q - 1) * half:k + q * half]
    count = jnp.concatenate([count_lo, picked[half:k]], axis=0)
    n_picked = jnp.sum(count, axis=0, keepdims=True)
    return count, z, n_picked


def _route_tile_exact(s0, s1, k):
    a, rank0, _ = _top_ranks(s0, k, True)
    b, rank1, _ = _top_ranks(s1, k, True)
    count, z, _ = _pair_counts(a, b, k, True)
    n0 = jnp.zeros_like(s0)
    for p in range(k):
        n0 = jnp.where(rank0 == float(p), count[p:p + 1], n0)
    e1 = jnp.exp(s1 - b[0:1]) / z
    e0 = jnp.exp(s0 - a[0:1])
    return rank1.astype(BF16), e1.astype(BF16), n0, e0


def _quick_pairs(a, b):
    tm = a.shape[1]
    pf = lax.broadcasted_iota(jnp.int32, (16, tm), 0)
    ph = lax.broadcasted_iota(jnp.int32, (8, tm), 0)
    ninf = -jnp.inf
    return jnp.concatenate([
        b + a[0:1],
        b[0:8] + a[1:2],
        jnp.where(pf >= 2, a + b[0:1], ninf),
        jnp.where(ph >= 2, a[0:8] + b[1:2], ninf),
        jnp.where((ph >= 2) & (ph <= 4), b[0:8] + a[2:3], ninf),
        jnp.where((ph >= 2) & (ph <= 3), b[0:8] + a[3:4], ninf),
        jnp.where(ph == 2, b[0:8] + a[4:5], ninf)], axis=0)


def _route_tile_quick(s0, s1, k):
    assert k == 16
    tm = s0.shape[1]
    kf = float(k)
    a, _, _ = _top_ranks(s0, k, False, want_rank=False)
    b, rank1, ranked1 = _top_ranks(s1, k, False)
    cand = _quick_pairs(a, b)
    z = jnp.zeros((1, tm), F32)
    best = m = None
    for r in range(k):
        m = jnp.max(cand, axis=0, keepdims=True)
        best = m if r == 0 else best
        cand = jnp.where(cand == m, -jnp.inf, cand)
        z = z + jnp.exp(m - best)
    picked = jnp.where(_quick_pairs(a, b) >= m, 1.0, 0.0)
    ph = lax.broadcasted_iota(jnp.int32, (8, tm), 0)
    count_lo = picked[24:32] + picked[40:48]
    for p, rows in enumerate([(0, 16), (16, 24), (48, 56), (56, 64), (64, 72)]):
        count_lo = count_lo + jnp.where(ph == p, jnp.sum(picked[rows[0]:rows[1]], axis=0, keepdims=True), 0.0)
    count = jnp.concatenate([count_lo, picked[32:40]], axis=0)
    n_picked = jnp.sum(count, axis=0, keepdims=True)
    n0 = jnp.zeros_like(s0)
    for p in range(k):
        n0 = jnp.where(s0 == a[p:p + 1], count[p:p + 1], n0)
    ranked0 = jnp.sum(jnp.where(s0 >= a[k - 1:k], 1.0, 0.0), axis=0, keepdims=True)
    e1 = jnp.exp(s1 - b[0:1]) / z
    e0 = jnp.exp(s0 - a[0:1])
    doubt = jnp.where((ranked0 == kf) & (ranked1 == kf) & (n_picked == kf), 0.0, 1.0)
    return rank1.astype(BF16), e1.astype(BF16), n0, e0, doubt


def _pair_scores(q_ref, keys_ref, pair, n_tiles):
    dq = keys_ref.shape[3]
    dn = (((1,), (1,)), ((), ()))
    h = pair // n_tiles
    q = q_ref[h, pl.ds(pl.multiple_of((pair % n_tiles) * LANES, LANES), LANES), :]
    s0 = lax.dot_general(keys_ref[h, 0], q[:, :dq], dn, preferred_element_type=F32)
    s1 = lax.dot_general(keys_ref[h, 1], q[:, dq:], dn, preferred_element_type=F32)
    return s0, s1


def _route_pair(s0, s1, pair, n_tiles, k, outs):
    h = pair // n_tiles
    lanes = pl.ds(pl.multiple_of((pair % n_tiles) * LANES, LANES), LANES)

    def write(vals):
        for ref, val in zip(outs, vals):
            ref[h, :, lanes] = val

    *vals, doubt = _route_tile_quick(s0, s1, k)
    write(vals)

    def redo():
        write(_route_tile_exact(s0, s1, k))

    return redo, jnp.max(doubt)


def _peer_dense_body(q0_ref, qn_ref, keys_ref, xt_ref, u_ref, vt_ref, o_ref, r1_s, e1_s, n0_s, e0_s,
                     *, rows, k, group):
    i = pl.program_id(0)
    s = pl.program_id(1)
    _, n_heads, n_keys, tb = r1_s.shape
    n_tiles = tb // LANES
    n_pairs = n_heads * n_tiles
    cur = i % 2
    slots = lambda b: [r1_s.at[b], e1_s.at[b], n0_s.at[b], e0_s.at[b]]
    pair_of = lambda a: jnp.minimum(s * group + a, n_pairs - 1)

    @pl.when(s == 0)
    def _():
        o_ref[...] = jnp.zeros_like(o_ref)

    @pl.when((i == 0) & (s == 0))
    def _():
        def first(pair, carry):
            s0, s1 = _pair_scores(q0_ref, keys_ref, pair, n_tiles)
            redo, flag = _route_pair(s0, s1, pair, n_tiles, k, slots(0))
            pl.when(flag > 0.0)(redo)
            return carry
        lax.fori_loop(0, n_pairs, first, 0)

    pending = []
    for a in range(group):
        s0, s1 = _pair_scores(qn_ref, keys_ref, pair_of(a), n_tiles)
        pending.append(_route_pair(s0, s1, pair_of(a), n_tiles, k, slots(1 - cur)))

    ht = jnp.dot(u_ref[...], xt_ref[...], preferred_element_type=F32)
    coef = []
    for r in range(rows):
        w = jnp.zeros((n_keys, tb), BF16)
        for h in range(n_heads):
            n_row = jnp.broadcast_to(n0_s[cur, h, pl.ds(s * rows + r, 1), :], (n_keys, tb)).astype(BF16)
            e_row = jnp.broadcast_to(e0_s[cur, h, pl.ds(s * rows + r, 1), :], (n_keys, tb)).astype(BF16)
            w = w + jnp.where(r1_s[cur, h] < n_row, e1_s[cur, h], jnp.zeros((), BF16)) * e_row
        hr = ht[r * n_keys:(r + 1) * n_keys]
        act = 0.5 * hr * (1.0 + lax.erf(hr * (1.0 / math.sqrt(2.0))))
        coef.append(act.astype(BF16) * w)
    coef = jnp.concatenate(coef, axis=0)
    o_ref[...] += jnp.dot(vt_ref[...], coef, preferred_element_type=F32)

    for redo, flag in pending:
        pl.when(flag > 0.0)(redo)


def _peer_dense(q3, keys, xt, u, vt, tb=512):
    d, t = xt.shape
    n_heads, _, n_keys, dq = keys.shape
    n_exp, _, eb = vt.shape
    rows = eb // n_keys
    tb = min(tb, t)
    assert tb % LANES == 0
    n_tok = t // tb
    group = -(-(n_heads * (tb // LANES)) // n_exp)
    slot_bf16 = pltpu.VMEM((2, n_heads, n_keys, tb), BF16)
    slot_f32 = pltpu.VMEM((2, n_heads, n_keys, tb), F32)
    return pl.pallas_call(
        functools.partial(_peer_dense_body, rows=rows, k=PEER_TOPK, group=group),
        grid=(n_tok, n_exp),
        in_specs=[pl.BlockSpec((n_heads, tb, 2 * dq), lambda i, s: (0, 0, 0)),
                  pl.BlockSpec((n_heads, tb, 2 * dq), lambda i, s: (0, jnp.minimum(i + 1, n_tok - 1), 0)),
                  pl.BlockSpec((n_heads, 2, n_keys, dq), lambda i, s: (0, 0, 0, 0)),
                  pl.BlockSpec((d, tb), lambda i, s: (0, i)),
                  pl.BlockSpec((eb, d), lambda i, s: (s, 0)),
                  pl.BlockSpec((None, d, eb), lambda i, s: (s, 0, 0))],
        out_specs=pl.BlockSpec((d, tb), lambda i, s: (0, i)),
        out_shape=jax.ShapeDtypeStruct((d, t), F32),
        scratch_shapes=[slot_bf16, slot_bf16, slot_f32, slot_f32],
        compiler_params=_cparams("arbitrary", "arbitrary"),
        name="peer_dense",
    )(q3, q3, keys, xt, u, vt)


def _ln_t_body(x_ref, yt_ref, g_ref, b_ref, o_ref, ob_ref, *, alpha):
    x2 = _layer_norm(alpha * x_ref[...] + yt_ref[...].T, g_ref[...], b_ref[...])
    o_ref[...] = x2
    ob_ref[...] = x2.astype(BF16)


def _ln_t(x, yt, g, b, alpha, tm=256):
    t, d = x.shape
    tm = min(tm, t)
    tok = pl.BlockSpec((tm, d), lambda i: (i, 0))
    vec = pl.BlockSpec((1, d), lambda i: (0, 0))
    return pl.pallas_call(
        functools.partial(_ln_t_body, alpha=alpha),
        grid=(t // tm,),
        in_specs=[tok, pl.BlockSpec((d, tm), lambda i: (0, i)), vec, vec],
        out_specs=[tok, tok],
        out_shape=[jax.ShapeDtypeStruct((t, d), F32), jax.ShapeDtypeStruct((t, d), BF16)],
        compiler_params=_cparams("parallel"),
        name="peer_ln",
    )(x, yt, g.reshape(1, d), b.reshape(1, d))


def _peer(x, xb, xt, w_q, sub_keys, u_tabs, v_tabs, layer, g, b, alpha):
    n_keys, dq = sub_keys.shape[2], sub_keys.shape[3]
    q3 = _matmul_heads(xb, w_q.astype(BF16), 2 * dq, name="peer_q")
    u = _table_bf16(u_tabs, layer)
    vt = _table_blocks_t_bf16(v_tabs, layer, PEER_BLOCK_ROWS * n_keys)
    yt = _peer_dense(q3, sub_keys.astype(BF16), xt, u, vt)
    return _ln_t(x, yt, g, b, alpha)


def _sb_attn_body(q_ref, k_ref, v_ref, o_ref, acc_ref, run_ref, *, tq, dh, hp, scale):
    qi = pl.program_id(2)
    row = lax.broadcasted_iota(jnp.int32, (tq, tq), 0)
    col = lax.broadcasted_iota(jnp.int32, (tq, tq), 1)
    later = jnp.where(row > col, 1.0, 0.0).astype(BF16)
    causal = col < row

    def block(a, kb, mask, run):
        start = pl.multiple_of(kb * tq, tq)
        lanes = slice(a * dh, (a + 1) * dh)
        kblk = k_ref[pl.ds(start, tq), lanes]
        vblk = v_ref[pl.ds(start, tq), lanes]
        z = lax.dot_general(q_ref[:, lanes], kblk, (((1,), (1,)), ((), ())), preferred_element_type=F32) * scale
        log_b = jnp.minimum(z, 0.0) - jnp.log(1.0 + jnp.exp(-jnp.abs(z)))
        log_1mb = log_b - z
        if mask is not None:
            log_1mb = jnp.where(mask, log_1mb, 0.0)
        hi = log_1mb.astype(BF16)
        lo = (log_1mb - hi.astype(F32)).astype(BF16)
        rest = (jnp.dot(hi, later, preferred_element_type=F32)
                + jnp.dot(lo, later, preferred_element_type=F32))
        att = jnp.exp(log_b + rest + run)
        if mask is not None:
            att = jnp.where(mask, att, 0.0)
        pv = jnp.dot(att.astype(BF16), vblk, preferred_element_type=F32)
        return pv, run + rest[:, 0:1] + log_1mb[:, 0:1]

    top = None
    for a in range(hp):
        pv0, run = block(a, qi, causal, jnp.zeros((tq, 1), F32))
        run = jnp.where(qi > 0, run, -jnp.inf)
        pv1, run = block(a, jnp.maximum(qi - 1, 0), None, run)
        acc_ref[a] = pv0 + pv1
        run_ref[a] = run
        top = jnp.max(run) if top is None else jnp.maximum(top, jnp.max(run))

    def cond(c):
        kb, top = c
        return (kb >= 0) & (top > SB_LOG_CUTOFF)

    def body(c):
        kb, _ = c
        top = None
        for a in range(hp):
            pv, run = block(a, kb, None, run_ref[a])
            acc_ref[a] += pv
            run_ref[a] = run
            top = jnp.max(run) if top is None else jnp.maximum(top, jnp.max(run))
        return kb - 1, top

    lax.while_loop(cond, body, (qi - 2, top))
    for a in range(hp):
        o_ref[:, a * dh:(a + 1) * dh] = acc_ref[a].astype(o_ref.dtype)


def _sb_attention(q3, kv3, n_heads, tq=256, hp=8):
    b, s, d = q3.shape
    dh = d // n_heads
    tq = min(tq, s)
    hp = min(hp, n_heads)
    groups = n_heads // hp
    return pl.pallas_call(
        functools.partial(_sb_attn_body, tq=tq, dh=dh, hp=hp, scale=dh ** -0.5),
        grid=(b, groups, s // tq),
        in_specs=[pl.BlockSpec((None, tq, hp * dh), lambda bi, h, i: (bi, i, h)),
                  pl.BlockSpec((None, s, hp * dh), lambda bi, h, i: (bi, 0, h)),
                  pl.BlockSpec((None, s, hp * dh), lambda bi, h, i: (bi, 0, h + groups))],
        out_specs=pl.BlockSpec((None, tq, hp * dh), lambda bi, h, i: (bi, i, h)),
        out_shape=jax.ShapeDtypeStruct((b, s, d), BF16),
        scratch_shapes=[pltpu.VMEM((hp, tq, dh), F32), pltpu.VMEM((hp, tq, 1), F32)],
        compiler_params=_cparams("parallel", "parallel", "arbitrary"),
        name="sb_attn",
    )(q3, kv3, kv3)


def kernel(x, conv_w_in, conv_b_in, conv_w_dw, conv_b_dw, conv_ln_g, conv_ln_b, conv_w_out, conv_b_out,
           attn_w_kv, attn_w_q, attn_w_o, peer_w_q, peer_sub_keys, peer_u, peer_v, ln_g, ln_b):
    bsz, seq, d = x.shape
    t = bsz * seq
    depth = peer_w_q.shape[0]
    n_a = conv_w_in.shape[0]
    alpha = (2.0 * depth) ** 0.25
    n_sb_heads = d // SB_HEAD_DIM

    xf = x.reshape(t, d)
    kv3 = None
    for i in range(depth):
        if i < n_a:
            g = _glu(xf, conv_w_in[i].astype(BF16), conv_b_in[i])
            c = _dwconv(g.reshape(bsz, seq, d), conv_w_dw[i].reshape(-1, d), conv_b_dw[i]).reshape(t, d)
            x1, x1b, x1t = _proj_ln(c, xf, conv_w_out[i].astype(BF16), conv_b_out[i], ln_g[i, 0], ln_b[i, 0],
                                    alpha, pre=(conv_ln_g[i], conv_ln_b[i]))
        else:
            j = i - n_a
            q = _matmul(xb, attn_w_q[j].astype(BF16), BF16, name="attn_q")
            o = _sb_attention(q.reshape(bsz, seq, d), kv3, n_sb_heads).reshape(t, d)
            x1, x1b, x1t = _proj_ln(o, xf, attn_w_o[j].astype(BF16), jnp.zeros((d,), F32), ln_g[i, 0],
                                    ln_b[i, 0], alpha)
        xf, xb = _peer(x1, x1b, x1t, peer_w_q[i], peer_sub_keys[i], peer_u, peer_v, i,
                       ln_g[i, 1], ln_b[i, 1], alpha)
        if i == n_a - 1:
            kv3 = _matmul(xb, attn_w_kv.astype(BF16), BF16, name="attn_kv").reshape(bsz, seq, 2 * d)
    return xf.reshape(bsz, seq, d)
```

```python
import functools
import math

import jax
import jax.numpy as jnp
from jax import lax
from jax.experimental import pallas as pl
from jax.experimental.pallas import tpu as pltpu

F32 = jnp.float32
BF16 = jnp.bfloat16

LN_EPS = 1e-5
LANES = 128
SUBLANES = 8
CONV_HALO = 32
PEER_TOPK = 16
PEER_BLOCK_ROWS = 8
SB_HEAD_DIM = 128
SB_LOG_CUTOFF = -104.0
VMEM_LIMIT_BYTES = 56 * 1024 * 1024


def _cparams(*sem):
    return pltpu.CompilerParams(dimension_semantics=sem, vmem_limit_bytes=VMEM_LIMIT_BYTES)


def _layer_norm(v, g, b):
    mu = jnp.mean(v, axis=-1, keepdims=True)
    c = v - mu
    var = jnp.mean(c * c, axis=-1, keepdims=True)
    return c * lax.rsqrt(var + LN_EPS) * g + b


def _mm_body(x_ref, w_ref, o_ref):
    o_ref[...] = jnp.dot(x_ref[...], w_ref[...], preferred_element_type=F32).astype(o_ref.dtype)


def _matmul(x, w, out_dtype, tm=1024, tn=1024, name="matmul"):
    m, k = x.shape
    n = w.shape[1]
    tm, tn = min(tm, m), min(tn, n)
    return pl.pallas_call(
        _mm_body,
        grid=(m // tm, n // tn),
        in_specs=[pl.BlockSpec((tm, k), lambda i, j: (i, 0)),
                  pl.BlockSpec((k, tn), lambda i, j: (0, j))],
        out_specs=pl.BlockSpec((tm, tn), lambda i, j: (i, j)),
        out_shape=jax.ShapeDtypeStruct((m, n), out_dtype),
        compiler_params=_cparams("parallel", "parallel"),
        name=name,
    )(x, w)


def _mm_heads_body(x_ref, w_ref, o_ref):
    hb, _, width = o_ref.shape
    res = jnp.dot(x_ref[...], w_ref[...], preferred_element_type=F32).astype(o_ref.dtype)
    for h in range(hb):
        o_ref[h] = res[:, h * width:(h + 1) * width]


def _matmul_heads(x, w, width, tm=1024, hb=4, name="matmul_heads"):
    m, k = x.shape
    n_heads = w.shape[1] // width
    tm = min(tm, m)
    hb = math.gcd(hb, n_heads)
    return pl.pallas_call(
        _mm_heads_body,
        grid=(m // tm, n_heads // hb),
        in_specs=[pl.BlockSpec((tm, k), lambda i, j: (i, 0)),
                  pl.BlockSpec((k, hb * width), lambda i, j: (0, j))],
        out_specs=pl.BlockSpec((hb, tm, width), lambda i, j: (j, i, 0)),
        out_shape=jax.ShapeDtypeStruct((n_heads, m, width), BF16),
        compiler_params=_cparams("parallel", "parallel"),
        name=name,
    )(x, w)


def _cast_body(x_ref, o_ref):
    o_ref[...] = x_ref[...].astype(o_ref.dtype)


def _cast_t_body(x_ref, o_ref):
    o_ref[...] = x_ref[...].T.astype(o_ref.dtype)


def _table_bf16(tab, layer, rows=1024):
    _, e, d = tab.shape
    rows = min(rows, e)
    return pl.pallas_call(
        _cast_body,
        grid=(e // rows,),
        in_specs=[pl.BlockSpec((None, rows, d), lambda j: (layer, j, 0))],
        out_specs=pl.BlockSpec((rows, d), lambda j: (j, 0)),
        out_shape=jax.ShapeDtypeStruct((e, d), BF16),
        compiler_params=_cparams("parallel"),
        name="table_bf16",
    )(tab)


def _table_blocks_t_bf16(tab, layer, eb, rows=512):
    _, e, d = tab.shape
    rows = min(rows, eb)
    per = eb // rows
    return pl.pallas_call(
        _cast_t_body,
        grid=(e // eb, per),
        in_specs=[pl.BlockSpec((None, rows, d), lambda j, c: (layer, j * per + c, 0))],
        out_specs=pl.BlockSpec((None, d, rows), lambda j, c: (j, 0, c)),
        out_shape=jax.ShapeDtypeStruct((e // eb, d, eb), BF16),
        compiler_params=_cparams("parallel", "parallel"),
        name="table_t_bf16",
    )(tab)


def _glu_body(x_ref, wa_ref, wg_ref, ba_ref, bg_ref, o_ref):
    xb = x_ref[...].astype(BF16)
    a = jnp.dot(xb, wa_ref[...], preferred_element_type=F32) + ba_ref[...]
    gate = jnp.dot(xb, wg_ref[...], preferred_element_type=F32) + bg_ref[...]
    o_ref[...] = a * jax.nn.sigmoid(gate)


def _glu(x, w_in, b_in, tm=1024, tn=512):
    t, d = x.shape
    tm, tn = min(tm, t), min(tn, d)
    nb = d // tn
    b2 = b_in.reshape(1, 2 * d)
    return pl.pallas_call(
        _glu_body,
        grid=(t // tm, nb),
        in_specs=[pl.BlockSpec((tm, d), lambda i, j: (i, 0)),
                  pl.BlockSpec((d, tn), lambda i, j: (0, j)),
                  pl.BlockSpec((d, tn), lambda i, j: (0, j + nb)),
                  pl.BlockSpec((1, tn), lambda i, j: (0, j)),
                  pl.BlockSpec((1, tn), lambda i, j: (0, j + nb))],
        out_specs=pl.BlockSpec((tm, tn), lambda i, j: (i, j)),
        out_shape=jax.ShapeDtypeStruct((t, d), F32),
        compiler_params=_cparams("parallel", "parallel"),
        name="conv_glu",
    )(x, w_in, w_in, b2, b2)


def _dwconv_body(halo_ref, cur_ref, w_ref, b_ref, o_ref, buf_ref, shift_ref, *, width, ts):
    i = pl.program_id(1)
    halo = halo_ref[...]
    buf_ref[0:CONV_HALO, :] = jnp.where(i > 0, halo, jnp.zeros_like(halo))
    buf_ref[CONV_HALO:CONV_HALO + ts, :] = cur_ref[...]
    acc = jnp.broadcast_to(b_ref[...], o_ref.shape)
    first = CONV_HALO - (width - 1)
    for r in range(min(SUBLANES, width)):
        taps = range(r, width, SUBLANES)
        span = ts + (len(taps) - 1) * SUBLANES
        shift_ref[r, 0:span, :] = buf_ref[first + r:first + r + span, :]
        for a, k in enumerate(taps):
            acc = acc + w_ref[k:k + 1, :] * shift_ref[r, a * SUBLANES:a * SUBLANES + ts, :]
    o_ref[...] = acc


def _dwconv(g3, w_dw, b_dw, ts=512, td=256):
    b, s, d = g3.shape
    width = w_dw.shape[0]
    assert width <= CONV_HALO
    ts, td = min(ts, s), min(td, d)
    per = ts // CONV_HALO
    return pl.pallas_call(
        functools.partial(_dwconv_body, width=width, ts=ts),
        grid=(b, s // ts, d // td),
        in_specs=[pl.BlockSpec((None, CONV_HALO, td), lambda bi, i, j: (bi, jnp.maximum(i * per - 1, 0), j)),
                  pl.BlockSpec((None, ts, td), lambda bi, i, j: (bi, i, j)),
                  pl.BlockSpec((width, td), lambda bi, i, j: (0, j)),
                  pl.BlockSpec((1, td), lambda bi, i, j: (0, j))],
        out_specs=pl.BlockSpec((None, ts, td), lambda bi, i, j: (bi, i, j)),
        out_shape=jax.ShapeDtypeStruct((b, s, d), F32),
        scratch_shapes=[pltpu.VMEM((CONV_HALO + ts, td), F32),
                        pltpu.VMEM((SUBLANES, ts + CONV_HALO - SUBLANES, td), F32)],
        compiler_params=_cparams("parallel", "parallel", "parallel"),
        name="conv_dw",
    )(g3, g3, w_dw, b_dw.reshape(1, d))


def _proj_ln_body(*refs, alpha, pre_ln):
    if pre_ln:
        h_ref, x_ref, w_ref, pg_ref, pb_ref, bias_ref, g_ref, b_ref, o_ref, ob_ref, ot_ref = refs
        h = _layer_norm(h_ref[...], pg_ref[...], pb_ref[...])
        h = (h * jax.nn.sigmoid(h)).astype(BF16)
    else:
        h_ref, x_ref, w_ref, bias_ref, g_ref, b_ref, o_ref, ob_ref, ot_ref = refs
        h = h_ref[...]
    y = jnp.dot(h, w_ref[...], preferred_element_type=F32) + bias_ref[...]
    x1 = _layer_norm(alpha * x_ref[...] + y, g_ref[...], b_ref[...])
    o_ref[...] = x1
    ob_ref[...] = x1.astype(BF16)
    ot_ref[...] = x1.T.astype(BF16)


def _proj_ln(h, x, w, bias, g, b, alpha, pre=None, tm=512):
    t, d = x.shape
    tm = min(tm, t)
    row = lambda v: v.reshape(1, d)
    tok = pl.BlockSpec((tm, d), lambda i: (i, 0))
    vec = pl.BlockSpec((1, d), lambda i: (0, 0))
    mat = pl.BlockSpec((d, d), lambda i: (0, 0))
    if pre is not None:
        ins = [h, x, w, row(pre[0]), row(pre[1]), row(bias), row(g), row(b)]
        specs = [tok, tok, mat, vec, vec, vec, vec, vec]
    else:
        ins = [h, x, w, row(bias), row(g), row(b)]
        specs = [tok, tok, mat, vec, vec, vec]
    return pl.pallas_call(
        functools.partial(_proj_ln_body, alpha=alpha, pre_ln=pre is not None),
        grid=(t // tm,),
        in_specs=specs,
        out_specs=[tok, tok, pl.BlockSpec((d, tm), lambda i: (0, i))],
        out_shape=[jax.ShapeDtypeStruct((t, d), F32), jax.ShapeDtypeStruct((t, d), BF16),
                   jax.ShapeDtypeStruct((d, t), BF16)],
        compiler_params=_cparams("parallel"),
        name="proj_ln",
    )(*ins)


def _top_ranks(s, k, exact, want_rank=True):
    n, tm = s.shape
    krow = lax.broadcasted_iota(jnp.int32, (k, tm), 0)
    rank = jnp.full((n, tm), float(k), F32)
    vals = jnp.zeros((k, tm), F32)
    if exact:
        row = lax.broadcasted_iota(jnp.int32, (n, tm), 0).astype(F32)
    for p in range(k):
        m = jnp.max(s, axis=0, keepdims=True)
        hit = s == m
        if exact:
            first = jnp.min(jnp.where(hit, row, float(n)), axis=0, keepdims=True)
            hit = row == first
        if want_rank:
            rank = jnp.where(hit, float(p), rank)
        s = jnp.where(hit, -jnp.inf, s)
        vals = jnp.where(krow == p, m, vals)
    ranked = jnp.sum(jnp.where(rank < float(k), 1.0, 0.0), axis=0, keepdims=True) if want_rank else None
    return vals, rank, ranked


def _pair_counts_exact(a, b, k):
    tm = a.shape[1]
    half = k // 2
    p_full = lax.broadcasted_iota(jnp.int32, (k, tm), 0).astype(F32)
    p_half = lax.broadcasted_iota(jnp.int32, (half, tm), 0).astype(F32)
    cand = [a + b[0:1]]
    flat = [p_full * k]
    for q in range(1, k):
        lim = k // (q + 1)
        cand.append(jnp.where(p_half < lim, a[0:half] + b[q:q + 1], -jnp.inf))
        flat.append(p_half * k + q)
    cand = jnp.concatenate(cand, axis=0)
    flat = jnp.concatenate(flat, axis=0)
    picked = jnp.zeros(cand.shape, F32)
    z = jnp.zeros((1, tm), F32)
    best = None
    for r in range(k):
        m = jnp.max(cand, axis=0, keepdims=True)
        best = m if r == 0 else best
        first = jnp.min(jnp.where(cand == m, flat, float(k * k)), axis=0, keepdims=True)
        hit = flat == first
        picked = jnp.where(hit, 1.0, picked)
        cand = jnp.where(hit, -jnp.inf, cand)
        z = z + jnp.exp(m - best)
    count_lo = picked[0:half]
    for q in range(1, k):
        count_lo = count_lo + picked[k + (q - 1) * half:k + q * half]
    return jnp.concatenate([count_lo, picked[half:k]], axis=0), z


def _route_tile_exact(s0, s1, k):
    a, rank0, _ = _top_ranks(s0, k, True)
    b, rank1, _ = _top_ranks(s1, k, True)
    count, z = _pair_counts_exact(a, b, k)
    n0 = jnp.zeros_like(s0)
    for p in range(k):
        n0 = jnp.where(rank0 == float(p), count[p:p + 1], n0)
    e1 = jnp.exp(s1 - b[0:1]) / z
    e0 = jnp.exp(s0 - a[0:1])
    return rank1.astype(BF16), e1.astype(BF16), n0, e0


def _quick_pairs(a, b):
    tm = a.shape[1]
    pf = lax.broadcasted_iota(jnp.int32, (16, tm), 0)
    ph = lax.broadcasted_iota(jnp.int32, (8, tm), 0)
    ninf = -jnp.inf
    return jnp.concatenate([
        b + a[0:1],
        b[0:8] + a[1:2],
        jnp.where(pf >= 2, a + b[0:1], ninf),
        jnp.where(ph >= 2, a[0:8] + b[1:2], ninf),
        jnp.where((ph >= 2) & (ph <= 4), b[0:8] + a[2:3], ninf),
        jnp.where((ph >= 2) & (ph <= 3), b[0:8] + a[3:4], ninf),
        jnp.where(ph == 2, b[0:8] + a[4:5], ninf)], axis=0)


def _route_tile_quick(s0, s1, k):
    assert k == 16
    tm = s0.shape[1]
    kf = float(k)
    a, _, _ = _top_ranks(s0, k, False, want_rank=False)
    b, rank1, ranked1 = _top_ranks(s1, k, False)
    cand = _quick_pairs(a, b)
    z = jnp.zeros((1, tm), F32)
    best = m = None
    for r in range(k):
        m = jnp.max(cand, axis=0, keepdims=True)
        best = m if r == 0 else best
        cand = jnp.where(cand == m, -jnp.inf, cand)
        z = z + jnp.exp(m - best)
    picked = jnp.where(_quick_pairs(a, b) >= m, 1.0, 0.0)
    ph = lax.broadcasted_iota(jnp.int32, (8, tm), 0)
    count_lo = picked[24:32] + picked[40:48]
    for p, rows in enumerate([(0, 16), (16, 24), (48, 56), (56, 64), (64, 72)]):
        count_lo = count_lo + jnp.where(ph == p, jnp.sum(picked[rows[0]:rows[1]], axis=0, keepdims=True), 0.0)
    count = jnp.concatenate([count_lo, picked[32:40]], axis=0)
    n_picked = jnp.sum(count, axis=0, keepdims=True)
    n0 = jnp.zeros_like(s0)
    for p in range(k):
        n0 = jnp.where(s0 == a[p:p + 1], count[p:p + 1], n0)
    ranked0 = jnp.sum(jnp.where(s0 >= a[k - 1:k], 1.0, 0.0), axis=0, keepdims=True)
    e1 = jnp.exp(s1 - b[0:1]) / z
    e0 = jnp.exp(s0 - a[0:1])
    doubt = jnp.where((ranked0 == kf) & (ranked1 == kf) & (n_picked == kf), 0.0, 1.0)
    return rank1.astype(BF16), e1.astype(BF16), n0, e0, doubt


def _pair_scores(q_ref, keys_ref, pair, n_tiles):
    dq = keys_ref.shape[3]
    dn = (((1,), (1,)), ((), ()))
    h = pair // n_tiles
    q = q_ref[h, pl.ds(pl.multiple_of((pair % n_tiles) * LANES, LANES), LANES), :]
    s0 = lax.dot_general(keys_ref[h, 0], q[:, :dq], dn, preferred_element_type=F32)
    s1 = lax.dot_general(keys_ref[h, 1], q[:, dq:], dn, preferred_element_type=F32)
    return s0, s1


def _route_pair(s0, s1, pair, n_tiles, k, outs):
    h = pair // n_tiles
    lanes = pl.ds(pl.multiple_of((pair % n_tiles) * LANES, LANES), LANES)

    def write(vals):
        for ref, val in zip(outs, vals):
            ref[h, :, lanes] = val

    *vals, doubt = _route_tile_quick(s0, s1, k)
    write(vals)

    def redo():
        write(_route_tile_exact(s0, s1, k))

    return redo, jnp.max(doubt)


def _peer_dense_body(q0_ref, qn_ref, keys_ref, xt_ref, u_ref, vt_ref, o_ref, r1_s, e1_s, n0_s, e0_s,
                     *, rows, k, group):
    i = pl.program_id(0)
    s = pl.program_id(1)
    _, n_heads, n_keys, tb = r1_s.shape
    n_tiles = tb // LANES
    n_pairs = n_heads * n_tiles
    cur = i % 2
    slots = lambda b: [r1_s.at[b], e1_s.at[b], n0_s.at[b], e0_s.at[b]]
    pair_of = lambda a: jnp.minimum(s * group + a, n_pairs - 1)

    @pl.when(s == 0)
    def _():
        o_ref[...] = jnp.zeros_like(o_ref)

    @pl.when((i == 0) & (s == 0))
    def _():
        def first(pair, carry):
            s0, s1 = _pair_scores(q0_ref, keys_ref, pair, n_tiles)
            redo, flag = _route_pair(s0, s1, pair, n_tiles, k, slots(0))
            pl.when(flag > 0.0)(redo)
            return carry
        lax.fori_loop(0, n_pairs, first, 0)

    pending = []
    for a in range(group):
        s0, s1 = _pair_scores(qn_ref, keys_ref, pair_of(a), n_tiles)
        pending.append(_route_pair(s0, s1, pair_of(a), n_tiles, k, slots(1 - cur)))

    ht = jnp.dot(u_ref[...], xt_ref[...], preferred_element_type=F32)
    coef = []
    for r in range(rows):
        w = jnp.zeros((n_keys, tb), BF16)
        for h in range(n_heads):
            n_row = jnp.broadcast_to(n0_s[cur, h, pl.ds(s * rows + r, 1), :], (n_keys, tb)).astype(BF16)
            e_row = jnp.broadcast_to(e0_s[cur, h, pl.ds(s * rows + r, 1), :], (n_keys, tb)).astype(BF16)
            w = w + jnp.where(r1_s[cur, h] < n_row, e1_s[cur, h], jnp.zeros((), BF16)) * e_row
        hr = ht[r * n_keys:(r + 1) * n_keys]
        act = 0.5 * hr * (1.0 + lax.erf(hr * (1.0 / math.sqrt(2.0))))
        coef.append(act.astype(BF16) * w)
    coef = jnp.concatenate(coef, axis=0)
    o_ref[...] += jnp.dot(vt_ref[...], coef, preferred_element_type=F32)

    for redo, flag in pending:
        pl.when(flag > 0.0)(redo)


def _peer_dense(q3, keys, xt, u, vt, tb=512):
    d, t = xt.shape
    n_heads, _, n_keys, dq = keys.shape
    n_exp, _, eb = vt.shape
    rows = eb // n_keys
    tb = min(tb, t)
    assert tb % LANES == 0
    n_tok = t // tb
    group = -(-(n_heads * (tb // LANES)) // n_exp)
    slot_bf16 = pltpu.VMEM((2, n_heads, n_keys, tb), BF16)
    slot_f32 = pltpu.VMEM((2, n_heads, n_keys, tb), F32)
    return pl.pallas_call(
        functools.partial(_peer_dense_body, rows=rows, k=PEER_TOPK, group=group),
        grid=(n_tok, n_exp),
        in_specs=[pl.BlockSpec((n_heads, tb, 2 * dq), lambda i, s: (0, 0, 0)),
                  pl.BlockSpec((n_heads, tb, 2 * dq), lambda i, s: (0, jnp.minimum(i + 1, n_tok - 1), 0)),
                  pl.BlockSpec((n_heads, 2, n_keys, dq), lambda i, s: (0, 0, 0, 0)),
                  pl.BlockSpec((d, tb), lambda i, s: (0, i)),
                  pl.BlockSpec((eb, d), lambda i, s: (s, 0)),
                  pl.BlockSpec((None, d, eb), lambda i, s: (s, 0, 0))],
        out_specs=pl.BlockSpec((d, tb), lambda i, s: (0, i)),
        out_shape=jax.ShapeDtypeStruct((d, t), F32),
        scratch_shapes=[slot_bf16, slot_bf16, slot_f32, slot_f32],
        compiler_params=_cparams("arbitrary", "arbitrary"),
        name="peer_dense",
    )(q3, q3, keys, xt, u, vt)


def _ln_t_body(x_ref, yt_ref, g_ref, b_ref, o_ref, ob_ref, *, alpha):
    x2 = _layer_norm(alpha * x_ref[...] + yt_ref[...].T, g_ref[...], b_ref[...])
    o_ref[...] = x2
    ob_ref[...] = x2.astype(BF16)


def _ln_t(x, yt, g, b, alpha, tm=256):
    t, d = x.shape
    tm = min(tm, t)
    tok = pl.BlockSpec((tm, d), lambda i: (i, 0))
    vec = pl.BlockSpec((1, d), lambda i: (0, 0))
    return pl.pallas_call(
        functools.partial(_ln_t_body, alpha=alpha),
        grid=(t // tm,),
        in_specs=[tok, pl.BlockSpec((d, tm), lambda i: (0, i)), vec, vec],
        out_specs=[tok, tok],
        out_shape=[jax.ShapeDtypeStruct((t, d), F32), jax.ShapeDtypeStruct((t, d), BF16)],
        compiler_params=_cparams("parallel"),
        name="peer_ln",
    )(x, yt, g.reshape(1, d), b.reshape(1, d))


def _peer(x, xb, xt, w_q, sub_keys, u_tabs, v_tabs, layer, g, b, alpha):
    n_keys, dq = sub_keys.shape[2], sub_keys.shape[3]
    q3 = _matmul_heads(xb, w_q.astype(BF16), 2 * dq, name="peer_q")
    u = _table_bf16(u_tabs, layer)
    vt = _table_blocks_t_bf16(v_tabs, layer, PEER_BLOCK_ROWS * n_keys)
    yt = _peer_dense(q3, sub_keys.astype(BF16), xt, u, vt)
    return _ln_t(x, yt, g, b, alpha)


def _sb_attn_body(q_ref, k_ref, v_ref, o_ref, acc_ref, run_ref, *, tq, dh, hp, scale):
    qi = pl.program_id(2)
    row = lax.broadcasted_iota(jnp.int32, (tq, tq), 0)
    col = lax.broadcasted_iota(jnp.int32, (tq, tq), 1)
    later = jnp.where(row > col, 1.0, 0.0).astype(BF16)
    causal = col < row

    def block(a, kb, mask, run):
        start = pl.multiple_of(kb * tq, tq)
        lanes = slice(a * dh, (a + 1) * dh)
        kblk = k_ref[pl.ds(start, tq), lanes]
        vblk = v_ref[pl.ds(start, tq), lanes]
        z = lax.dot_general(q_ref[:, lanes], kblk, (((1,), (1,)), ((), ())), preferred_element_type=F32) * scale
        log_b = jnp.minimum(z, 0.0) - jnp.log(1.0 + jnp.exp(-jnp.abs(z)))
        log_1mb = log_b - z
        if mask is not None:
            log_1mb = jnp.where(mask, log_1mb, 0.0)
        hi = log_1mb.astype(BF16)
        lo = (log_1mb - hi.astype(F32)).astype(BF16)
        rest = (jnp.dot(hi, later, preferred_element_type=F32)
                + jnp.dot(lo, later, preferred_element_type=F32))
        att = jnp.exp(log_b + rest + run)
        if mask is not None:
            att = jnp.where(mask, att, 0.0)
        pv = jnp.dot(att.astype(BF16), vblk, preferred_element_type=F32)
        return pv, run + rest[:, 0:1] + log_1mb[:, 0:1]

    top = None
    for a in range(hp):
        pv0, run = block(a, qi, causal, jnp.zeros((tq, 1), F32))
        run = jnp.where(qi > 0, run, -jnp.inf)
        pv1, run = block(a, jnp.maximum(qi - 1, 0), None, run)
        acc_ref[a] = pv0 + pv1
        run_ref[a] = run
        top = jnp.max(run) if top is None else jnp.maximum(top, jnp.max(run))

    def cond(c):
        kb, top = c
        return (kb >= 0) & (top > SB_LOG_CUTOFF)

    def body(c):
        kb, _ = c
        top = None
        for a in range(hp):
            pv, run = block(a, kb, None, run_ref[a])
            acc_ref[a] += pv
            run_ref[a] = run
            top = jnp.max(run) if top is None else jnp.maximum(top, jnp.max(run))
        return kb - 1, top

    lax.while_loop(cond, body, (qi - 2, top))
    for a in range(hp):
        o_ref[:, a * dh:(a + 1) * dh] = acc_ref[a].astype(o_ref.dtype)


def _sb_attention(q3, kv3, n_heads, tq=256, hp=8):
    b, s, d = q3.shape
    dh = d // n_heads
    tq = min(tq, s)
    hp = min(hp, n_heads)
    groups = n_heads // hp
    return pl.pallas_call(
        functools.partial(_sb_attn_body, tq=tq, dh=dh, hp=hp, scale=dh ** -0.5),
        grid=(b, groups, s // tq),
        in_specs=[pl.BlockSpec((None, tq, hp * dh), lambda bi, h, i: (bi, i, h)),
                  pl.BlockSpec((None, s, hp * dh), lambda bi, h, i: (bi, 0, h)),
                  pl.BlockSpec((None, s, hp * dh), lambda bi, h, i: (bi, 0, h + groups))],
        out_specs=pl.BlockSpec((None, tq, hp * dh), lambda bi, h, i: (bi, i, h)),
        out_shape=jax.ShapeDtypeStruct((b, s, d), BF16),
        scratch_shapes=[pltpu.VMEM((hp, tq, dh), F32), pltpu.VMEM((hp, tq, 1), F32)],
        compiler_params=_cparams("parallel", "parallel", "arbitrary"),
        name="sb_attn",
    )(q3, kv3, kv3)


def kernel(x, conv_w_in, conv_b_in, conv_w_dw, conv_b_dw, conv_ln_g, conv_ln_b, conv_w_out, conv_b_out,
           attn_w_kv, attn_w_q, attn_w_o, peer_w_q, peer_sub_keys, peer_u, peer_v, ln_g, ln_b):
    bsz, seq, d = x.shape
    t = bsz * seq
    depth = peer_w_q.shape[0]
    n_a = conv_w_in.shape[0]
    alpha = (2.0 * depth) ** 0.25
    n_sb_heads = d // SB_HEAD_DIM

    xf = x.reshape(t, d)
    kv3 = None
    for i in range(depth):
        if i < n_a:
            g = _glu(xf, conv_w_in[i].astype(BF16), conv_b_in[i])
            c = _dwconv(g.reshape(bsz, seq, d), conv_w_dw[i].reshape(-1, d), conv_b_dw[i]).reshape(t, d)
            x1, x1b, x1t = _proj_ln(c, xf, conv_w_out[i].astype(BF16), conv_b_out[i], ln_g[i, 0], ln_b[i, 0],
                                    alpha, pre=(conv_ln_g[i], conv_ln_b[i]))
        else:
            j = i - n_a
            q = _matmul(xb, attn_w_q[j].astype(BF16), BF16, name="attn_q")
            o = _sb_attention(q.reshape(bsz, seq, d), kv3, n_sb_heads).reshape(t, d)
            x1, x1b, x1t = _proj_ln(o, xf, attn_w_o[j].astype(BF16), jnp.zeros((d,), F32), ln_g[i, 0],
                                    ln_b[i, 0], alpha)
        xf, xb = _peer(x1, x1b, x1t, peer_w_q[i], peer_sub_keys[i], peer_u, peer_v, i,
                       ln_g[i, 1], ln_b[i, 1], alpha)
        if i == n_a - 1:
            kv3 = _matmul(xb, attn_w_kv.astype(BF16), BF16, name="attn_kv").reshape(bsz, seq, 2 * d)
    return xf.reshape(bsz, seq, d)
```
